```python
import math
import jax, jax.numpy as jnp
from jax import lax
import numpy as np


D_MODEL = 2048
BATCH = 16
SEQ = 2048
DEPTH = 1

CHUNK = 64
EPS = 1e-6
SSM_WIDTH = D_MODEL // 4
SSM_GROUP = 16
SSM_GROUPS = SSM_WIDTH // SSM_GROUP
SSM_STATE = 64
MLSTM_WIDTH = D_MODEL - SSM_WIDTH
MLSTM_HEADS = 4
MLSTM_HEAD_DIM = MLSTM_WIDTH // MLSTM_HEADS
CONV_WIDTH = 4
PEER_HEADS = 8
PEER_QUERY_DIM = 256
PEER_NKEYS = 128
PEER_TOPK = 16
PEER_EXPERTS = PEER_NKEYS * PEER_NKEYS
PEER_TOKEN_BLOCK = 128
IN_SPLITS = (SSM_WIDTH, MLSTM_WIDTH, MLSTM_WIDTH, MLSTM_HEADS, MLSTM_HEADS, D_MODEL, D_MODEL)
IN_WIDTH = sum(IN_SPLITS)
IN_OFFSETS = tuple(int(o) for o in np.cumsum(IN_SPLITS)[:-1])

kernel_name = 'hybrid_s5_mlstm_peer_block'


def rmsnorm(x, g):
    xf = x.astype(jnp.float32)
    y = xf * lax.rsqrt(jnp.mean(xf * xf, axis=-1, keepdims=True) + EPS)
    return (y * g.astype(jnp.float32)).astype(x.dtype)


def s5_branch(u, lam_re, lam_im, log_dt, b_re, b_im, c_re, c_im, d_skip, w_glu):
    bsz, seq, _ = u.shape
    f32 = jnp.float32
    uf = u.astype(f32).reshape(bsz, seq, SSM_GROUPS, SSM_GROUP)
    lam = lax.complex(lam_re.astype(f32), lam_im.astype(f32))
    dt = jnp.exp(log_dt.astype(f32))[:, None]
    a_bar = jnp.exp(lam * dt)
    b_bar = ((a_bar - 1.0) / lam)[..., None] * lax.complex(b_re.astype(f32), b_im.astype(f32))
    bu = jnp.einsum('gph,bsgh->bsgp', b_bar, uf.astype(jnp.complex64))
    a_full = jnp.broadcast_to(a_bar, bu.shape)

    def combine(left, right):
        a_l, b_l = left
        a_r, b_r = right
        return a_r * a_l, a_r * b_l + b_r

    _, states = lax.associative_scan(combine, (a_full, bu), axis=1)
    c = lax.complex(c_re.astype(f32), c_im.astype(f32))
    y = jnp.real(jnp.einsum('ghp,bsgp->bsgh', c, states)) + d_skip.astype(f32).reshape(SSM_GROUPS, SSM_GROUP) * uf
    y = jax.nn.gelu(y.reshape(bsz, seq, SSM_WIDTH)).astype(u.dtype)
    val, gate = jnp.split(y @ w_glu, 2, axis=-1)
    return val * jax.nn.sigmoid(gate)


def causal_conv(x, w, b):
    seq = x.shape[1]
    xp = jnp.pad(x, ((0, 0), (CONV_WIDTH - 1, 0), (0, 0)))
    out = b
    for j in range(CONV_WIDTH):
        out = out + xp[:, j:j + seq] * w[j]
    return out


def _to_chunks(t):
    b, s = t.shape[:2]
    t = t.reshape((b, s // CHUNK, CHUNK) + t.shape[2:])
    t = jnp.swapaxes(t, 2, 3)
    return jnp.moveaxis(t, 1, 0)


def mlstm_chunkwise(q, k, v, i_pre, logf):
    bsz, seq, nh, dk = q.shape
    dv = v.shape[-1]
    f32 = jnp.float32
    causal = jnp.tril(jnp.ones((CHUNK, CHUNK), dtype=bool))

    def step(carry, inp):
        c_prev, n_prev, m_prev = carry
        qc, kc, vc, ic, fc = inp
        bcum = jnp.cumsum(fc, axis=-1)
        log_w = bcum[..., :, None] - bcum[..., None, :] + ic[..., None, :]
        log_w = jnp.where(causal, log_w, -jnp.inf)
        log_inter = bcum + m_prev[..., None]
        m_t = jnp.maximum(log_inter, jnp.max(log_w, axis=-1))
        w = jnp.exp(log_w - m_t[..., None])
        inter = jnp.exp(log_inter - m_t)
        sw = jnp.einsum('bhtd,bhsd->bhts', qc, kc) * w
        num = jnp.einsum('bhts,bhse->bhte', sw, vc) + inter[..., None] * jnp.einsum('bhed,bhtd->bhte', c_prev, qc)
        den = jnp.sum(sw, axis=-1) + inter * jnp.einsum('bhd,bhtd->bht', n_prev, qc)
        h = num / jnp.maximum(jnp.abs(den), jnp.exp(-m_t))[..., None]
        b_last = bcum[..., -1]
        m_new = m_t[..., -1]
        w_end = jnp.exp(b_last[..., None] - bcum + ic - m_new[..., None])
        decay = jnp.exp(b_last + m_prev - m_new)
        c_new = decay[..., None, None] * c_prev + jnp.einsum('bhs,bhse,bhsd->bhed', w_end, vc, kc)
        n_new = decay[..., None] * n_prev + jnp.einsum('bhs,bhsd->bhd', w_end, kc)
        return (c_new, n_new, m_new), h

    init = (jnp.zeros((bsz, nh, dv, dk), f32), jnp.zeros((bsz, nh, dk), f32), jnp.zeros((bsz, nh), f32))
    xs = (_to_chunks(q), _to_chunks(k), _to_chunks(v), _to_chunks(i_pre), _to_chunks(logf))
    _, hs = lax.scan(step, init, xs)
    hs = jnp.swapaxes(jnp.moveaxis(hs, 0, 1), 2, 3)
    return hs.reshape(bsz, seq, nh, dv)


def mlstm_branch(xm, z, i_raw, f_raw, conv_w, conv_b, w_q, w_k, w_v, b_i, b_f, mh_gain, skip, w_proj):
    bsz, seq, _ = xm.shape
    f32 = jnp.float32
    heads = lambda t: t.reshape(bsz, seq, MLSTM_HEADS, MLSTM_HEAD_DIM)
    xc = jax.nn.silu(causal_conv(xm, conv_w, conv_b))
    q = jnp.einsum('bshd,hde->bshe', heads(xc), w_q).astype(f32)
    k = (jnp.einsum('bshd,hde->bshe', heads(xc), w_k) * MLSTM_HEAD_DIM ** -0.5).astype(f32)
    v = jnp.einsum('bshd,hde->bshe', heads(xm), w_v).astype(f32)
    i_pre = (i_raw + b_i).astype(f32)
    logf = jax.nn.log_sigmoid((f_raw + b_f).astype(f32))
    h = mlstm_chunkwise(q, k, v, i_pre, logf)
    h = jax.nn.sigmoid(heads(z).astype(f32)) * h
    mu = jnp.mean(h, axis=-1, keepdims=True)
    var = jnp.mean(jnp.square(h - mu), axis=-1, keepdims=True)
    h = ((h - mu) * lax.rsqrt(var + EPS)).reshape(bsz, seq, MLSTM_WIDTH)
    h = h * mh_gain.astype(f32) + skip.astype(f32) * xc.astype(f32)
    return h.astype(xm.dtype) @ w_proj


def peer(h, w_query, key1, key2, expert_u, expert_v):
    bsz, seq, d = h.shape
    n_tok = bsz * seq
    hk = PEER_HEADS * PEER_TOPK
    t = h.reshape(n_tok, d)
    qr = (t @ w_query).reshape(n_tok, PEER_HEADS, PEER_QUERY_DIM)
    q1, q2 = jnp.split(qr, 2, axis=-1)
    s1 = jnp.einsum('thd,hnd->thn', q1, key1).astype(jnp.float32)
    s2 = jnp.einsum('thd,hnd->thn', q2, key2).astype(jnp.float32)
    v1, i1 = lax.top_k(s1, PEER_TOPK)
    v2, i2 = lax.top_k(s2, PEER_TOPK)
    cand = (v1[..., :, None] + v2[..., None, :]).reshape(n_tok, PEER_HEADS, PEER_TOPK * PEER_TOPK)
    cand_idx = (i1[..., :, None] * PEER_NKEYS + i2[..., None, :]).reshape(n_tok, PEER_HEADS, PEER_TOPK * PEER_TOPK)
    top_s, pos = lax.top_k(cand, PEER_TOPK)
    experts = jnp.take_along_axis(cand_idx, pos, axis=-1)
    gates = jax.nn.softmax(top_s, axis=-1).astype(h.dtype)

    def block(args):
        tb, eb, gb = args
        u = jnp.take(expert_u, eb, axis=0)
        act = jax.nn.gelu(jnp.einsum('ted,td->te', u, tb)) * gb
        vv = jnp.take(expert_v, eb, axis=0)
        return jnp.einsum('te,ted->td', act, vv)

    nb = n_tok // PEER_TOKEN_BLOCK
    out = lax.map(block, (t.reshape(nb, PEER_TOKEN_BLOCK, d),
                          experts.reshape(nb, PEER_TOKEN_BLOCK, hk),
                          gates.reshape(nb, PEER_TOKEN_BLOCK, hk)))
    return out.reshape(bsz, seq, d)


def setup_inputs(seed: int = 0) -> dict:
    key = jax.random.key(seed)
    keys = jax.random.split(key, 40)
    cnt = [0]
    f32 = jnp.float32

    def nk():
        cnt[0] += 1
        return keys[cnt[0] - 1]

    def nrm(shape, scale):
        return jax.random.normal(nk(), shape, f32) * scale

    L = DEPTH
    G, P, H = SSM_GROUPS, SSM_STATE, SSM_GROUP
    hd = MLSTM_HEAD_DIM
    half = PEER_QUERY_DIM // 2
    return {
        'x': nrm((BATCH, SEQ, D_MODEL), 1.0),
        'norm1_g': 1.0 + nrm((L, D_MODEL), 0.02),
        'w_in': nrm((L, D_MODEL, IN_WIDTH), D_MODEL ** -0.5),
        'lam_re': -0.5 + nrm((L, G, P), 0.01),
        'lam_im': jnp.pi * jnp.arange(P, dtype=f32) + nrm((L, G, P), 0.01),
        'log_dt': jax.random.uniform(nk(), (L, G), f32, math.log(1e-3), math.log(1e-1)),
        'b_re': nrm((L, G, P, H), (2 * H) ** -0.5),
        'b_im': nrm((L, G, P, H), (2 * H) ** -0.5),
        'c_re': nrm((L, G, H, P), (2 * P) ** -0.5),
        'c_im': nrm((L, G, H, P), (2 * P) ** -0.5),
        'd_skip': nrm((L, SSM_WIDTH), 1.0),
        'w_glu': nrm((L, SSM_WIDTH, 2 * D_MODEL), SSM_WIDTH ** -0.5),
        'conv_w': nrm((L, CONV_WIDTH, MLSTM_WIDTH), 0.5),
        'conv_b': nrm((L, MLSTM_WIDTH), 0.02),
        'w_q': nrm((L, MLSTM_HEADS, hd, hd), hd ** -0.5),
        'w_k': nrm((L, MLSTM_HEADS, hd, hd), hd ** -0.5),
        'w_v': nrm((L, MLSTM_HEADS, hd, hd), hd ** -0.5),
        'b_i': nrm((L, MLSTM_HEADS), 0.1),
        'b_f': jnp.linspace(3.0, 6.0, MLSTM_HEADS, dtype=f32) + nrm((L, MLSTM_HEADS), 0.1),
        'mh_gain': 1.0 + nrm((L, MLSTM_WIDTH), 0.02),
        'mlstm_skip': 1.0 + nrm((L, MLSTM_WIDTH), 0.02),
        'w_mlstm_out': nrm((L, MLSTM_WIDTH, D_MODEL), MLSTM_WIDTH ** -0.5),
        'w_out': nrm((L, D_MODEL, D_MODEL), D_MODEL ** -0.5),
        'norm2_g': 1.0 + nrm((L, D_MODEL), 0.02),
        'w_query': nrm((L, D_MODEL, PEER_HEADS * PEER_QUERY_DIM), D_MODEL ** -0.5),
        'key1': nrm((L, PEER_HEADS, PEER_NKEYS, half), half ** -0.5),
        'key2': nrm((L, PEER_HEADS, PEER_NKEYS, half), half ** -0.5),
        'expert_u': nrm((L, PEER_EXPERTS, D_MODEL), D_MODEL ** -0.5),
        'expert_v': nrm((L, PEER_EXPERTS, D_MODEL), (PEER_HEADS * PEER_TOPK) ** -0.5),
        'norm_f_g': 1.0 + nrm((D_MODEL,), 0.02),
    }


def reference(x, norm1_g, w_in, lam_re, lam_im, log_dt, b_re, b_im, c_re, c_im, d_skip, w_glu,
              conv_w, conv_b, w_q, w_k, w_v, b_i, b_f, mh_gain, mlstm_skip, w_mlstm_out, w_out,
              norm2_g, w_query, key1, key2, expert_u, expert_v, norm_f_g):
    h = x
    for l in range(DEPTH):
        hn = rmsnorm(h, norm1_g[l])
        proj = hn @ w_in[l]
        u_ssm, xm, z, i_raw, f_raw, g_a, g_b = jnp.split(proj, IN_OFFSETS, axis=-1)
        y_a = s5_branch(u_ssm, lam_re[l], lam_im[l], log_dt[l], b_re[l], b_im[l],
                        c_re[l], c_im[l], d_skip[l], w_glu[l])
        y_b = mlstm_branch(xm, z, i_raw, f_raw, conv_w[l], conv_b[l], w_q[l], w_k[l], w_v[l],
                           b_i[l], b_f[l], mh_gain[l], mlstm_skip[l], w_mlstm_out[l])
        merged = jax.nn.sigmoid(g_a) * y_a + jax.nn.sigmoid(g_b) * y_b
        h = h + merged @ w_out[l]
        h = h + peer(rmsnorm(h, norm2_g[l]), w_query[l], key1[l], key2[l], expert_u[l], expert_v[l])
    return rmsnorm(h, norm_f_g)
```

```python
import functools
import math

import jax
import jax.numpy as jnp
import numpy as np
from jax import lax
from jax.experimental import pallas as pl
from jax.experimental.pallas import tpu as pltpu

F32 = jnp.float32
BF16 = jnp.bfloat16
EPS = 1e-6

D_MODEL = 2048
SSM_WIDTH = 512
SSM_GROUP = 16
SSM_GROUPS = 32
SSM_STATE = 64
N_STATE = SSM_GROUPS * SSM_STATE
MLSTM_WIDTH = 1536
HEADS = 4
HEAD_DIM = 384
CONV_WIDTH = 4
CHUNK = 64
PEER_HEADS = 8
PEER_HALF = 128
PEER_NKEYS = 128
PEER_TOPK = 16
PEER_EXPERTS = PEER_NKEYS * PEER_NKEYS
HK = PEER_HEADS * PEER_TOPK

LANES = 128
SUBLANES = 8
ROW_TILES = D_MODEL // LANES

COL_XM = 0
COL_Z = 1536
COL_SSM = 3072
COL_GATE = 3584
COL_GA = 4096
COL_GB = 6144
PROJ_W = 8192

INPROJ_TM = 1024
INPROJ_TN = 1024
S5_LT = 32
MLSTM_TT = 256
MERGE_TM = 256
ROUTE_TB = 256
PEER_EC = 2048
PEER_NCHUNK = PEER_EXPERTS // PEER_EC
PEER_SHIFT = int(math.log2(PEER_EC))
PEER_TB = 256
PEER_SB = 32
PEER_G = 8
FINAL_TM = 512

VMEM_LIMIT = 56 * 1024 * 1024


def _cparams(sem):
    return pltpu.CompilerParams(dimension_semantics=sem, vmem_limit_bytes=VMEM_LIMIT)


def _const_spec(shape):
    nd = len(shape)
    return pl.BlockSpec(shape, lambda *_: (0,) * nd, pipeline_mode=pl.Buffered(1))


def _inproj_kernel(x_ref, g_ref, w_ref, o_ref, hn_ref):
    @pl.when(pl.program_id(1) == 0)
    def _():
        x = x_ref[...]
        y = x * lax.rsqrt(jnp.mean(x * x, axis=-1, keepdims=True) + EPS) * g_ref[...]
        hn_ref[...] = y.astype(BF16)

    o_ref[...] = jnp.dot(hn_ref[...], w_ref[...], preferred_element_type=F32)


def _inproj(x2, g, w_cat):
    t = x2.shape[0]
    tm = min(INPROJ_TM, t)
    return pl.pallas_call(
        _inproj_kernel,
        grid=(t // tm, PROJ_W // INPROJ_TN),
        in_specs=[
            pl.BlockSpec((tm, D_MODEL), lambda i, n: (i, 0)),
            pl.BlockSpec((1, D_MODEL), lambda i, n: (0, 0)),
            pl.BlockSpec((D_MODEL, INPROJ_TN), lambda i, n: (0, n)),
        ],
        out_specs=pl.BlockSpec((tm, INPROJ_TN), lambda i, n: (i, n)),
        out_shape=jax.ShapeDtypeStruct((t, PROJ_W), F32),
        scratch_shapes=[pltpu.VMEM((tm, D_MODEL), BF16)],
        compiler_params=_cparams(("arbitrary", "arbitrary")),
        name="inproj",
    )(x2, g, w_cat)


def _s5_kernel(u_ref, bblk_ref, are_ref, aim_ref, cblk_ref, d_ref, wglu_ref, o_ref, x_ref, st_ref, *, nb, lt):
    @pl.when(pl.program_id(0) == 0)
    def _():
        st_ref[...] = jnp.zeros_like(st_ref)

    rows = nb * lt
    u = u_ref[...].reshape(rows, SSM_WIDTH)
    bu = jnp.dot(u.astype(BF16), bblk_ref[...], preferred_element_type=F32)
    n_tiles = N_STATE // LANES
    for j in range(2 * n_tiles):
        x_ref[j] = bu[:, j * LANES:(j + 1) * LANES]

    tiles_per_pass = 4
    for t0 in range(0, n_tiles, tiles_per_pass):
        tiles = range(t0, t0 + tiles_per_pass)
        a_re = [are_ref[:, j * LANES:(j + 1) * LANES] for j in tiles]
        a_im = [aim_ref[:, j * LANES:(j + 1) * LANES] for j in tiles]

        def step(s, carry):
            r = pl.ds(s, nb, stride=lt)
            out = []
            for i, j in enumerate(tiles):
                s_re, s_im = carry[2 * i], carry[2 * i + 1]
                n_re = a_re[i] * s_re - a_im[i] * s_im + x_ref[j, r, :]
                n_im = a_re[i] * s_im + a_im[i] * s_re + x_ref[n_tiles + j, r, :]
                x_ref[j, r, :] = n_re
                x_ref[n_tiles + j, r, :] = n_im
                out += [n_re, n_im]
            return tuple(out)

        init = []
        for j in tiles:
            init += [st_ref[j], st_ref[n_tiles + j]]
        fin = lax.fori_loop(0, lt, step, tuple(init))
        for i, j in enumerate(tiles):
            st_ref[j] = fin[2 * i]
            st_ref[n_tiles + j] = fin[2 * i + 1]

    xs = jnp.concatenate([x_ref[j].astype(BF16) for j in range(2 * n_tiles)], axis=-1)
    y = jnp.dot(xs, cblk_ref[...], preferred_element_type=F32) + d_ref[...] * u
    y = jax.nn.gelu(y)
    vg = jnp.dot(y.astype(BF16), wglu_ref[...], preferred_element_type=F32)
    out = vg[:, :D_MODEL] * jax.nn.sigmoid(vg[:, D_MODEL:])
    o_ref[...] = out.reshape(nb, lt, D_MODEL)


def _s5(proj3, bblk, a_re, a_im, cblk, d_skip, w_glu):
    nb, seq, _ = proj3.shape
    lt = min(S5_LT, seq)
    kern = functools.partial(_s5_kernel, nb=nb, lt=lt)
    return pl.pallas_call(
        kern,
        grid=(seq // lt,),
        in_specs=[
            pl.BlockSpec((nb, lt, SSM_WIDTH), lambda c: (0, c, COL_SSM // SSM_WIDTH)),
            _const_spec((SSM_WIDTH, 2 * N_STATE)),
            _const_spec((1, N_STATE)),
            _const_spec((1, N_STATE)),
            _const_spec((2 * N_STATE, SSM_WIDTH)),
            _const_spec((1, SSM_WIDTH)),
            _const_spec((SSM_WIDTH, 2 * D_MODEL)),
        ],
        out_specs=pl.BlockSpec((nb, lt, D_MODEL), lambda c: (0, c, 0)),
        out_shape=jax.ShapeDtypeStruct((nb, seq, D_MODEL), F32),
        scratch_shapes=[pltpu.VMEM((2 * N_STATE // LANES, nb * lt, LANES), F32),
                        pltpu.VMEM((2 * N_STATE // LANES, nb, LANES), F32)],
        compiler_params=_cparams(("arbitrary",)),
        name="s5",
    )(proj3, bblk, a_re, a_im, cblk, d_skip, w_glu)


def _s5_params(lam_re, lam_im, log_dt, b_re, b_im, c_re, c_im):
    g, p, h = SSM_GROUPS, SSM_STATE, SSM_GROUP
    lam = lax.complex(lam_re.astype(F32), lam_im.astype(F32))
    dt = jnp.exp(log_dt.astype(F32))[:, None]
    a_bar = jnp.exp(lam * dt)
    b_bar = ((a_bar - 1.0) / lam)[..., None] * lax.complex(b_re.astype(F32), b_im.astype(F32))
    eye = jnp.eye(g, dtype=F32)
    bb_re = jnp.einsum("gph,gk->ghkp", jnp.real(b_bar), eye).reshape(g * h, g * p)
    bb_im = jnp.einsum("gph,gk->ghkp", jnp.imag(b_bar), eye).reshape(g * h, g * p)
    bblk = jnp.concatenate([bb_re, bb_im], axis=1).astype(BF16)
    cc_re = jnp.einsum("ghp,gk->kpgh", c_re.astype(F32), eye).reshape(g * p, g * h)
    cc_im = jnp.einsum("ghp,gk->kpgh", c_im.astype(F32), eye).reshape(g * p, g * h)
    cblk = jnp.concatenate([cc_re, -cc_im], axis=0).astype(BF16)
    return bblk, jnp.real(a_bar).reshape(1, g * p), jnp.imag(a_bar).reshape(1, g * p), cblk


def _mlstm_kernel(xm_ref, z_ref, gt_ref, cw_ref, cb_ref, wq_ref, wk_ref, wv_ref, gb_ref, gain_ref, skip_ref,
                  wp_ref, o_ref, cbuf_ref, c_ref, n_ref, m_ref, h_ref, *, tt):
    halo = SUBLANES

    @pl.when(pl.program_id(1) == 0)
    def _():
        cbuf_ref[0:halo, :] = jnp.zeros((halo, MLSTM_WIDTH), F32)
        c_ref[...] = jnp.zeros_like(c_ref)
        n_ref[...] = jnp.zeros_like(n_ref)
        m_ref[...] = jnp.zeros_like(m_ref)

    @pl.when(pl.program_id(1) > 0)
    def _():
        cbuf_ref[0:halo, :] = cbuf_ref[tt:tt + halo, :]

    xm = xm_ref[0]
    cbuf_ref[halo:halo + tt, :] = xm
    conv = cb_ref[...]
    for j in range(CONV_WIDTH):
        off = halo - (CONV_WIDTH - 1) + j
        conv = conv + cbuf_ref[off:off + tt, :] * cw_ref[j:j + 1, :]
    xc = conv * jax.nn.sigmoid(conv)
    xcb = xc.astype(BF16)
    xmb = xm.astype(BF16)

    row_i = lax.broadcasted_iota(jnp.int32, (CHUNK, CHUNK), 0)
    col_i = lax.broadcasted_iota(jnp.int32, (CHUNK, CHUNK), 1)
    causal = col_i <= row_i
    tril = causal.astype(F32)
    triu = (row_i <= col_i).astype(F32)

    qs, ks, vs = [], [], []
    for h in range(HEADS):
        hs = slice(h * HEAD_DIM, (h + 1) * HEAD_DIM)
        qs.append(jnp.dot(xcb[:, hs], wq_ref[h], preferred_element_type=F32))
        ks.append(jnp.dot(xcb[:, hs], wk_ref[h], preferred_element_type=F32) * (HEAD_DIM ** -0.5))
        vs.append(jnp.dot(xmb[:, hs], wv_ref[h], preferred_element_type=F32))

    for j in range(tt // CHUNK):
        rs = slice(j * CHUNK, (j + 1) * CHUNK)
        gcol = gt_ref[0, rs, :] + gb_ref[...]
        lf_col = jax.nn.log_sigmoid(gcol)
        bcum_col = jnp.dot(tril, lf_col, preferred_element_type=F32, precision=lax.Precision.HIGHEST)
        grow = gcol.T
        lf_row = jax.nn.log_sigmoid(grow[0:SUBLANES, :])
        bcum_row = jnp.dot(lf_row, triu, preferred_element_type=F32, precision=lax.Precision.HIGHEST)
        for h in range(HEADS):
            hs = slice(h * HEAD_DIM, (h + 1) * HEAD_DIM)
            qc, kc, vc = qs[h][rs], ks[h][rs], vs[h][rs]
            bc = bcum_col[:, HEADS + h:HEADS + h + 1]
            br = bcum_row[HEADS + h:HEADS + h + 1, :]
            ic = gcol[:, h:h + 1]
            ir = grow[h:h + 1, :]
            m_prev = m_ref[h:h + 1, 0:1]
            log_w = jnp.where(causal, bc - br + ir, -jnp.inf)
            log_inter = bc + m_prev
            m_t = jnp.maximum(log_inter, jnp.max(log_w, axis=-1, keepdims=True))
            w = jnp.exp(log_w - m_t)
            inter = jnp.exp(log_inter - m_t)
            qcb, kcb, vcb = qc.astype(BF16), kc.astype(BF16), vc.astype(BF16)
            s = lax.dot_general(qcb, kcb, (((1,), (1,)), ((), ())), preferred_element_type=F32)
            sw = s * w
            c_prev = c_ref[h]
            n_prev = n_ref[h:h + 1, :]
            qcmem = lax.dot_general(qcb, c_prev.astype(BF16), (((1,), (1,)), ((), ())),
                                    preferred_element_type=F32)
            num = jnp.dot(sw.astype(BF16), vcb, preferred_element_type=F32) + inter * qcmem
            den = jnp.sum(sw, axis=-1, keepdims=True) + inter * jnp.sum(qc * n_prev, axis=-1, keepdims=True)
            hh = num / jnp.maximum(jnp.abs(den), jnp.exp(-m_t))
            h_ref[rs, hs] = hh
            b_last = bc[CHUNK - 1:CHUNK, :]
            m_new = m_t[CHUNK - 1:CHUNK, :]
            w_end = jnp.exp(b_last - bc + ic - m_new)
            decay = jnp.exp(b_last + m_prev - m_new)
            vw = (vc * w_end).astype(BF16)
            c_ref[h] = decay * c_prev + lax.dot_general(vw, kcb, (((0,), (0,)), ((), ())),
                                                        preferred_element_type=F32)
            n_ref[h:h + 1, :] = decay * n_prev + jnp.sum(w_end * kc, axis=0, keepdims=True)
            m_ref[h:h + 1, :] = jnp.broadcast_to(m_new, (1, LANES))

    z = z_ref[0]
    outs = []
    for h in range(HEADS):
        hs = slice(h * HEAD_DIM, (h + 1) * HEAD_DIM)
        hg = jax.nn.sigmoid(z[:, hs]) * h_ref[:, hs]
        mu = jnp.mean(hg, axis=-1, keepdims=True)
        dv = hg - mu
        var = jnp.mean(dv * dv, axis=-1, keepdims=True)
        outs.append(dv * lax.rsqrt(var + EPS))
    hn = jnp.concatenate(outs, axis=-1)
    hn = hn * gain_ref[...] + skip_ref[...] * xc
    o_ref[0] = jnp.dot(hn.astype(BF16), wp_ref[...], preferred_element_type=F32)


def _mlstm(proj3, conv_w, conv_b, w_q, w_k, w_v, gate_bias, gain, skip, w_proj):
    nb, seq, _ = proj3.shape
    tt = min(MLSTM_TT, seq)
    kern = functools.partial(_mlstm_kernel, tt=tt)
    return pl.pallas_call(
        kern,
        grid=(nb, seq // tt),
        in_specs=[
            pl.BlockSpec((1, tt, MLSTM_WIDTH), lambda b, t: (b, t, COL_XM // MLSTM_WIDTH)),
            pl.BlockSpec((1, tt, MLSTM_WIDTH), lambda b, t: (b, t, COL_Z // MLSTM_WIDTH)),
            pl.BlockSpec((1, tt, LANES), lambda b, t: (b, t, COL_GATE // LANES)),
            _const_spec((CONV_WIDTH, MLSTM_WIDTH)),
            _const_spec((1, MLSTM_WIDTH)),
            _const_spec((HEADS, HEAD_DIM, HEAD_DIM)),
            _const_spec((HEADS, HEAD_DIM, HEAD_DIM)),
            _const_spec((HEADS, HEAD_DIM, HEAD_DIM)),
            _const_spec((1, LANES)),
            _const_spec((1, MLSTM_WIDTH)),
            _const_spec((1, MLSTM_WIDTH)),
            _const_spec((MLSTM_WIDTH, D_MODEL)),
        ],
        out_specs=pl.BlockSpec((1, tt, D_MODEL), lambda b, t: (b, t, 0)),
        out_shape=jax.ShapeDtypeStruct((nb, seq, D_MODEL), F32),
        scratch_shapes=[
            pltpu.VMEM((tt + SUBLANES, MLSTM_WIDTH), F32),
            pltpu.VMEM((HEADS, HEAD_DIM, HEAD_DIM), F32),
            pltpu.VMEM((SUBLANES, HEAD_DIM), F32),
            pltpu.VMEM((SUBLANES, LANES), F32),
            pltpu.VMEM((tt, MLSTM_WIDTH), F32),
        ],
        compiler_params=_cparams(("arbitrary", "arbitrary")),
        name="mlstm",
    )(proj3, proj3, proj3, conv_w, conv_b, w_q, w_k, w_v, gate_bias, gain, skip, w_proj)


def _merge_kernel(ga_ref, gb_ref, ya_ref, yb_ref, x_ref, wo_ref, g2_ref, wq_ref, h1_ref, hn_ref, q_ref):
    merged = jax.nn.sigmoid(ga_ref[...]) * ya_ref[...] + jax.nn.sigmoid(gb_ref[...]) * yb_ref[...]
    h1 = x_ref[...] + jnp.dot(merged.astype(BF16), wo_ref[...], preferred_element_type=F32)
    h1_ref[...] = h1
    hn = h1 * lax.rsqrt(jnp.mean(h1 * h1, axis=-1, keepdims=True) + EPS) * g2_ref[...]
    hn_ref[...] = hn
    q_ref[...] = jnp.dot(hn.astype(BF16), wq_ref[...], preferred_element_type=F32)


def _merge(proj, y_a, y_b, x2, w_out, g2, w_query):
    t = x2.shape[0]
    tm = min(MERGE_TM, t)
    row = lambda c: pl.BlockSpec((tm, D_MODEL), lambda i: (i, c))
    out_sd = jax.ShapeDtypeStruct((t, D_MODEL), F32)
    return pl.pallas_call(
        _merge_kernel,
        grid=(t // tm,),
        in_specs=[row(COL_GA // D_MODEL), row(COL_GB // D_MODEL), row(0), row(0), row(0),
                  _const_spec((D_MODEL, D_MODEL)), _const_spec((1, D_MODEL)), _const_spec((D_MODEL, D_MODEL))],
        out_specs=[row(0), row(0), row(0)],
        out_shape=[out_sd, out_sd, out_sd],
        compiler_params=_cparams(("arbitrary",)),
        name="merge",
    )(proj, proj, y_a, y_b, x2, w_out, g2, w_query)


_CAND_ROWS = [PEER_TOPK // (i + 1) for i in range(PEER_TOPK)]
_CAND_OFFS = [int(v) for v in np.cumsum([0] + _CAND_ROWS[:-1])]
_NCAND = int(sum(_CAND_ROWS))
_NCAND_PAD = 56


def _extract_topk(s, payload, k, val_ref, pay_ref):
    n = s.shape[0]
    big = jnp.int32(2 ** 30)
    for r in range(k):
        m = jnp.max(s, axis=0, keepdims=True)
        sel = jnp.min(jnp.where(s == m, payload, big), axis=0, keepdims=True)
        val_ref[r:r + 1, :] = m
        pay_ref[r:r + 1, :] = sel
        s = jnp.where(payload == sel, -jnp.inf, s)


def _route_kernel(q_ref, k1_ref, k2_ref, e_ref, g_ref, st_ref,
                  v1_ref, i1_ref, v2_ref, i2_ref, cv_ref, ce_ref, cp_ref, tv_ref, tp_ref, ea_ref, gate_ref, *, tb):
    key_iota = lax.broadcasted_iota(jnp.int32, (PEER_NKEYS, tb), 0)
    cand_iota = lax.broadcasted_iota(jnp.int32, (_NCAND_PAD, tb), 0)
    for h in range(PEER_HEADS):
        q1 = q_ref[:, h * 2 * PEER_HALF:h * 2 * PEER_HALF + PEER_HALF].astype(BF16)
        q2 = q_ref[:, h * 2 * PEER_HALF + PEER_HALF:(h + 1) * 2 * PEER_HALF].astype(BF16)
        s1 = lax.dot_general(k1_ref[h], q1, (((1,), (1,)), ((), ())), preferred_element_type=F32)
        s2 = lax.dot_general(k2_ref[h], q2, (((1,), (1,)), ((), ())), preferred_element_type=F32)
        _extract_topk(s1, key_iota, PEER_TOPK, v1_ref, i1_ref)
        _extract_topk(s2, key_iota, PEER_TOPK, v2_ref, i2_ref)
        cv_ref[...] = jnp.full((_NCAND_PAD, tb), -jnp.inf, F32)
        ce_ref[...] = jnp.zeros((_NCAND_PAD, tb), jnp.int32)
        cp_ref[...] = cand_iota + jnp.int32(1 << 20)
        for i in range(PEER_TOPK):
            n_i, off = _CAND_ROWS[i], _CAND_OFFS[i]
            cv_ref[off:off + n_i, :] = v1_ref[i:i + 1, :] + v2_ref[0:n_i, :]
            ce_ref[off:off + n_i, :] = i1_ref[i:i + 1, :] * PEER_NKEYS + i2_ref[0:n_i, :]
            cp_ref[off:off + n_i, :] = i * PEER_TOPK + lax.broadcasted_iota(jnp.int32, (n_i, tb), 0)
        cv = cv_ref[...]
        cp = cp_ref[...]
        _extract_topk(cv, cp, PEER_TOPK, tv_ref, tp_ref)
        ce = ce_ref[...]
        tv = tv_ref[...]
        ex = jnp.exp(tv - tv[0:1, :])
        gate_ref[h * PEER_TOPK:(h + 1) * PEER_TOPK, :] = ex / jnp.sum(ex, axis=0, keepdims=True)
        for r in range(PEER_TOPK):
            ea_ref[h * PEER_TOPK + r:h * PEER_TOPK + r + 1, :] = jnp.sum(
                jnp.where(cp == tp_ref[r:r + 1, :], ce, 0), axis=0, keepdims=True)

    e_all = ea_ref[...]
    g_all = gate_ref[...]
    chunk = e_all >> PEER_SHIFT
    ri = lax.broadcasted_iota(jnp.int32, (HK, HK), 0)
    ci = lax.broadcasted_iota(jnp.int32, (HK, HK), 1)
    lower = (ci < ri).astype(BF16)
    dest = jnp.zeros((HK, tb), F32)
    start = jnp.zeros((1, tb), F32)
    st_ref[...] = jnp.zeros(st_ref.shape, jnp.int32)
    for c in range(PEER_NCHUNK):
        mask = chunk == c
        maskf = mask.astype(F32)
        rank = jnp.dot(lower, mask.astype(BF16), preferred_element_type=F32)
        dest = dest + maskf * (start + rank)
        st_ref[c:c + 1, :] = start.astype(jnp.int32)
        start = start + jnp.sum(maskf, axis=0, keepdims=True)
    st_ref[PEER_NCHUNK:PEER_NCHUNK + 1, :] = start.astype(jnp.int32)
    dest_i = dest.astype(jnp.int32)
    for k in range(HK):
        sel = dest_i == k
        e_ref[k:k + 1, :] = jnp.sum(jnp.where(sel, e_all, 0), axis=0, keepdims=True)
        g_ref[k:k + 1, :] = jnp.sum(jnp.where(sel, g_all, 0.0), axis=0, keepdims=True)


def _route(qr, key1, key2):
    t = qr.shape[0]
    tb = min(ROUTE_TB, t)
    kern = functools.partial(_route_kernel, tb=tb)
    st_rows = 16
    assert PEER_NCHUNK + 1 <= st_rows
    col = lambda r: pl.BlockSpec((r, tb), lambda i: (0, i))
    f_s = lambda r: pltpu.VMEM((r, tb), F32)
    i_s = lambda r: pltpu.VMEM((r, tb), jnp.int32)
    return pl.pallas_call(
        kern,
        grid=(t // tb,),
        in_specs=[pl.BlockSpec((tb, D_MODEL), lambda i: (i, 0)),
                  _const_spec((PEER_HEADS, PEER_NKEYS, PEER_HALF)),
                  _const_spec((PEER_HEADS, PEER_NKEYS, PEER_HALF))],
        out_specs=[col(HK), col(HK), col(st_rows)],
        out_shape=[jax.ShapeDtypeStruct((HK, t), jnp.int32), jax.ShapeDtypeStruct((HK, t), F32),
                   jax.ShapeDtypeStruct((st_rows, t), jnp.int32)],
        scratch_shapes=[f_s(PEER_TOPK), i_s(PEER_TOPK), f_s(PEER_TOPK), i_s(PEER_TOPK),
                        f_s(_NCAND_PAD), i_s(_NCAND_PAD), i_s(_NCAND_PAD),
                        f_s(PEER_TOPK), i_s(PEER_TOPK), i_s(HK), f_s(HK)],
        compiler_params=_cparams(("arbitrary",)),
        name="route",
    )(qr, key1, key2)


def _peer_kernel(c_ref, e_hbm, g_hbm, st_hbm, t_ref, u_ref, v_ref, acc_ref, o_ref,
                 e_sm, g_sm, st_sm, sems, p_ref, s_ref, a_ref, *, tb):
    c = c_ref[0]
    b = pl.program_id(0)
    cp_e = pltpu.make_async_copy(e_hbm.at[pl.ds(b * tb, tb)], e_sm, sems.at[0])
    cp_g = pltpu.make_async_copy(g_hbm.at[pl.ds(b * tb, tb)], g_sm, sems.at[1])
    cp_s = pltpu.make_async_copy(st_hbm.at[pl.ds(b * tb * 16, tb * 16)], st_sm, sems.at[2])
    cp_e.start()
    cp_g.start()
    cp_s.start()
    cp_e.wait()
    cp_g.wait()
    cp_s.wait()
    base = c * PEER_EC
    rows8 = PEER_G * SUBLANES

    def pair_index(tok, s1, p):
        valid = p < s1
        e = e_sm[tok, jnp.minimum(p, HK - 1)] - base
        return valid, jnp.where(valid, e, 0)

    def sub_block(sb, _):
        tok0 = sb * PEER_SB

        def u_tok(i, gcount):
            tok = tok0 + i
            s0 = st_sm[tok * 16 + c]
            s1 = st_sm[tok * 16 + c + 1]
            ng = (s1 - s0 + (PEER_G - 1)) // PEER_G
            tvec = t_ref[tok]

            def u_grp(g, gc):
                for k in range(PEER_G):
                    _, e = pair_index(tok, s1, s0 + g * PEER_G + k)
                    prod = tvec * u_ref[e]
                    p_ref[k * SUBLANES:(k + 1) * SUBLANES, :] = prod[0:SUBLANES] + prod[SUBLANES:]
                r = p_ref[pl.ds(0, PEER_G, stride=SUBLANES), :]
                for s in range(1, SUBLANES):
                    r = r + p_ref[pl.ds(s, PEER_G, stride=SUBLANES), :]
                s_ref[gc] = r
                return gc + 1

            return lax.fori_loop(0, ng, u_grp, gcount)

        gtotal = lax.fori_loop(0, PEER_SB, u_tok, jnp.int32(0))

        def act_grp(g, _):
            sc = jnp.sum(s_ref[g], axis=-1, keepdims=True)
            a_ref[g] = jnp.broadcast_to(jax.nn.gelu(sc), (PEER_G, LANES))
            return 0

        lax.fori_loop(0, gtotal, act_grp, 0)

        def v_tok(i, gcount):
            tok = tok0 + i
            s0 = st_sm[tok * 16 + c]
            s1 = st_sm[tok * 16 + c + 1]
            ng = (s1 - s0 + (PEER_G - 1)) // PEER_G

            def v_grp(g, carry):
                gc, acc = carry
                for k in range(PEER_G):
                    p = s0 + g * PEER_G + k
                    valid, e = pair_index(tok, s1, p)
                    gate = jnp.where(valid, g_sm[tok, jnp.minimum(p, HK - 1)], 0.0)
                    a = a_ref[gc, pl.ds(k, 1), :] * gate
                    acc = acc + a * v_ref[e]
                return gc + 1, acc

            gcount, acc = lax.fori_loop(0, ng, v_grp, (gcount, acc_ref[tok]))
            o_ref[tok] = acc
            return gcount

        lax.fori_loop(0, PEER_SB, v_tok, jnp.int32(0))
        return 0

    lax.fori_loop(0, tb // PEER_SB, sub_block, 0)


def _peer(e_sorted, g_sorted, starts, t3, u3, v3, acc3):
    t = t3.shape[0]
    tb = min(PEER_TB, t)
    kern = functools.partial(_peer_kernel, tb=tb)
    gmax = PEER_SB * (HK // PEER_G)
    tok_spec = pl.BlockSpec((tb, ROW_TILES, LANES), lambda b, c: (b, 0, 0))
    tab_spec = pl.BlockSpec((PEER_EC, ROW_TILES, LANES), lambda b, c: (c[0], 0, 0), pipeline_mode=pl.Buffered(1))
    call = pl.pallas_call(
        kern,
        grid_spec=pltpu.PrefetchScalarGridSpec(
            num_scalar_prefetch=1,
            grid=(t // tb,),
            in_specs=[pl.BlockSpec(memory_space=pl.ANY), pl.BlockSpec(memory_space=pl.ANY),
                      pl.BlockSpec(memory_space=pl.ANY), tok_spec, tab_spec, tab_spec, tok_spec],
            out_specs=tok_spec,
            scratch_shapes=[
                pltpu.SMEM((tb, HK), jnp.int32),
                pltpu.SMEM((tb, HK), F32),
                pltpu.SMEM((tb * 16,), jnp.int32),
                pltpu.SemaphoreType.DMA((3,)),
                pltpu.VMEM((PEER_G * SUBLANES, LANES), F32),
                pltpu.VMEM((gmax, PEER_G, LANES), F32),
                pltpu.VMEM((gmax, PEER_G, LANES), F32),
            ],
        ),
        out_shape=jax.ShapeDtypeStruct(acc3.shape, F32),
        input_output_aliases={7: 0},
        compiler_params=_cparams(("arbitrary",)),
        name="peer",
    )
    acc = acc3
    for c in range(PEER_NCHUNK):
        acc = call(jnp.full((1,), c, jnp.int32), e_sorted, g_sorted, starts, t3, u3, v3, acc)
    return acc


def _final_kernel(a_ref, g_ref, o_ref):
    a = a_ref[...]
    ms = jnp.mean(a * a, axis=-1, keepdims=True)
    o_ref[...] = a * lax.rsqrt(ms + EPS) * g_ref[...]


def _final(acc2, g):
    t = acc2.shape[0]
    tm = min(FINAL_TM, t)
    return pl.pallas_call(
        _final_kernel,
        grid=(t // tm,),
        in_specs=[pl.BlockSpec((tm, D_MODEL), lambda i: (i, 0)), pl.BlockSpec((1, D_MODEL), lambda i: (0, 0))],
        out_specs=pl.BlockSpec((tm, D_MODEL), lambda i: (i, 0)),
        out_shape=jax.ShapeDtypeStruct((t, D_MODEL), F32),
        compiler_params=_cparams(("arbitrary",)),
        name="final",
    )(acc2, g)


def _layer(x, norm1_g, w_in, lam_re, lam_im, log_dt, b_re, b_im, c_re, c_im, d_skip, w_glu, conv_w, conv_b,
           w_q, w_k, w_v, b_i, b_f, mh_gain, mlstm_skip, w_mlstm_out, w_out, norm2_g, w_query, key1, key2,
           expert_u, expert_v):
    nb, seq, d = x.shape
    t = nb * seq
    x2 = x.reshape(t, d)
    row = lambda a: a.reshape(1, -1).astype(F32)

    o_ssm, o_xm, o_z, o_i, o_f, o_ga, o_gb = 0, 512, 2048, 3584, 3588, 3592, 5640
    zeros = lambda n: jnp.zeros((d, n), w_in.dtype)
    w_cat = jnp.concatenate([
        w_in[:, o_xm:o_z], w_in[:, o_z:o_i], w_in[:, o_ssm:o_xm], w_in[:, o_i:o_ga],
        zeros(COL_GA - COL_GATE - 2 * HEADS), w_in[:, o_ga:o_gb], w_in[:, o_gb:]], axis=1).astype(BF16)
    proj = _inproj(x2, row(norm1_g), w_cat)
    proj3 = proj.reshape(nb, seq, PROJ_W)

    bblk, a_re, a_im, cblk = _s5_params(lam_re, lam_im, log_dt, b_re, b_im, c_re, c_im)
    y_a = _s5(proj3, bblk, a_re, a_im, cblk, row(d_skip), w_glu.astype(BF16))

    gate_bias = jnp.concatenate([b_i.astype(F32), b_f.astype(F32), jnp.zeros((LANES - 2 * HEADS,), F32)]).reshape(1, LANES)
    y_b = _mlstm(proj3, conv_w.astype(F32), row(conv_b), w_q.astype(BF16), w_k.astype(BF16), w_v.astype(BF16),
                 gate_bias, row(mh_gain), row(mlstm_skip), w_mlstm_out.astype(BF16))

    h1, hn2, qr = _merge(proj, y_a.reshape(t, d), y_b.reshape(t, d), x2, w_out.astype(BF16), row(norm2_g),
                         w_query.astype(BF16))

    e_t, g_t, st_t = _route(qr, key1.astype(BF16), key2.astype(BF16))
    e_sorted = e_t.T
    g_sorted = g_t.T
    starts = st_t.T.reshape(t * 16)

    dense = lambda a: a.reshape(a.shape[0], ROW_TILES, LANES)
    acc = _peer(e_sorted, g_sorted, starts, dense(hn2), dense(expert_u.astype(F32)), dense(expert_v.astype(F32)),
                dense(h1))
    return acc.reshape(nb, seq, d)


def kernel(x, norm1_g, w_in, lam_re, lam_im, log_dt, b_re, b_im, c_re, c_im, d_skip, w_glu, conv_w, conv_b, w_q, w_k, w_v, b_i, b_f, mh_gain, mlstm_skip, w_mlstm_out, w_out, norm2_g, w_query, key1, key2, expert_u, expert_v, norm_f_g):
    depth = w_in.shape[0]
    h = x
    for l in range(depth):
        h = _layer(h, norm1_g[l], w_in[l], lam_re[l], lam_im[l], log_dt[l], b_re[l], b_im[l], c_re[l], c_im[l],
                   d_skip[l], w_glu[l], conv_w[l], conv_b[l], w_q[l], w_k[l], w_v[l], b_i[l], b_f[l], mh_gain[l],
                   mlstm_skip[l], w_mlstm_out[l], w_out[l], norm2_g[l], w_query[l], key1[l], key2[l],
                   expert_u[l], expert_v[l])
    nb, seq, d = h.shape
    out = _final(h.reshape(nb * seq, d), norm_f_g.reshape(1, d).astype(F32))
    return out.reshape(nb, seq, d)
```

```python
import functools
import math

import jax
import jax.numpy as jnp
import numpy as np
from jax import lax
from jax.experimental import pallas as pl
from jax.experimental.pallas import tpu as pltpu

F32 = jnp.float32
BF16 = jnp.bfloat16
EPS = 1e-6

D_MODEL = 2048
SSM_WIDTH = 512
SSM_GROUP = 16
SSM_GROUPS = 32
SSM_STATE = 64
N_STATE = SSM_GROUPS * SSM_STATE
MLSTM_WIDTH = 1536
HEADS = 4
HEAD_DIM = 384
CONV_WIDTH = 4
CHUNK = 64
PEER_HEADS = 8
PEER_HALF = 128
PEER_NKEYS = 128
PEER_TOPK = 16
PEER_EXPERTS = PEER_NKEYS * PEER_NKEYS
HK = PEER_HEADS * PEER_TOPK

LANES = 128
SUBLANES = 8
ROW_TILES = D_MODEL // LANES

COL_XM = 0
COL_Z = 1536
COL_SSM = 3072
COL_GATE = 3584
COL_GA = 4096
COL_GB = 6144
PROJ_W = 8192

INPROJ_TM = 1024
INPROJ_TN = 1024
S5_LT = 32
MLSTM_TT = 256
MERGE_TM = 256
ROUTE_ST = 512
PEER_EC = 4096
PEER_NCHUNK = PEER_EXPERTS // PEER_EC
PEER_SHIFT = int(math.log2(PEER_EC))
PEER_TB = 256
PEER_G = 8
PEER_SLOTS = 160
PEER_UNROLL = 4
FINAL_TM = 512
assert PEER_G == SUBLANES and HK + PEER_NCHUNK * (PEER_G - 1) <= PEER_SLOTS

VMEM_LIMIT = 56 * 1024 * 1024


def _cparams(sem):
    return pltpu.CompilerParams(dimension_semantics=sem, vmem_limit_bytes=VMEM_LIMIT)


def _const_spec(shape):
    nd = len(shape)
    return pl.BlockSpec(shape, lambda *_: (0,) * nd, pipeline_mode=pl.Buffered(1))


def _inproj_kernel(x_ref, g_ref, w_ref, o_ref, hn_ref):
    @pl.when(pl.program_id(1) == 0)
    def _():
        x = x_ref[...]
        y = x * lax.rsqrt(jnp.mean(x * x, axis=-1, keepdims=True) + EPS) * g_ref[...]
        hn_ref[...] = y.astype(BF16)

    o_ref[...] = jnp.dot(hn_ref[...], w_ref[...], preferred_element_type=F32)


def _inproj(x2, g, w_cat):
    t = x2.shape[0]
    tm = min(INPROJ_TM, t)
    return pl.pallas_call(
        _inproj_kernel,
        grid=(t // tm, PROJ_W // INPROJ_TN),
        in_specs=[
            pl.BlockSpec((tm, D_MODEL), lambda i, n: (i, 0)),
            pl.BlockSpec((1, D_MODEL), lambda i, n: (0, 0)),
            pl.BlockSpec((D_MODEL, INPROJ_TN), lambda i, n: (0, n)),
        ],
        out_specs=pl.BlockSpec((tm, INPROJ_TN), lambda i, n: (i, n)),
        out_shape=jax.ShapeDtypeStruct((t, PROJ_W), F32),
        scratch_shapes=[pltpu.VMEM((tm, D_MODEL), BF16)],
        compiler_params=_cparams(("arbitrary", "arbitrary")),
        name="inproj",
    )(x2, g, w_cat)


def _s5_kernel(u_ref, bblk_ref, are_ref, aim_ref, cblk_ref, d_ref, wglu_ref, o_ref, x_ref, st_ref, *, nb, lt):
    @pl.when(pl.program_id(0) == 0)
    def _():
        st_ref[...] = jnp.zeros_like(st_ref)

    rows = nb * lt
    u = u_ref[...].reshape(rows, SSM_WIDTH)
    bu = jnp.dot(u.astype(BF16), bblk_ref[...], preferred_element_type=F32)
    n_tiles = N_STATE // LANES
    for j in range(2 * n_tiles):
        x_ref[j] = bu[:, j * LANES:(j + 1) * LANES]

    tiles_per_pass = 4
    for t0 in range(0, n_tiles, tiles_per_pass):
        tiles = range(t0, t0 + tiles_per_pass)
        a_re = [are_ref[:, j * LANES:(j + 1) * LANES] for j in tiles]
        a_im = [aim_ref[:, j * LANES:(j + 1) * LANES] for j in tiles]

        def step(s, carry):
            r = pl.ds(s, nb, stride=lt)
            out = []
            for i, j in enumerate(tiles):
                s_re, s_im = carry[2 * i], carry[2 * i + 1]
                n_re = a_re[i] * s_re - a_im[i] * s_im + x_ref[j, r, :]
                n_im = a_re[i] * s_im + a_im[i] * s_re + x_ref[n_tiles + j, r, :]
                x_ref[j, r, :] = n_re
                x_ref[n_tiles + j, r, :] = n_im
                out += [n_re, n_im]
            return tuple(out)

        init = []
        for j in tiles:
            init += [st_ref[j], st_ref[n_tiles + j]]
        fin = lax.fori_loop(0, lt, step, tuple(init))
        for i, j in enumerate(tiles):
            st_ref[j] = fin[2 * i]
            st_ref[n_tiles + j] = fin[2 * i + 1]

    xs = jnp.concatenate([x_ref[j].astype(BF16) for j in range(2 * n_tiles)], axis=-1)
    y = jnp.dot(xs, cblk_ref[...], preferred_element_type=F32) + d_ref[...] * u
    y = jax.nn.gelu(y)
    vg = jnp.dot(y.astype(BF16), wglu_ref[...], preferred_element_type=F32)
    out = vg[:, :D_MODEL] * jax.nn.sigmoid(vg[:, D_MODEL:])
    o_ref[...] = out.reshape(nb, lt, D_MODEL)


def _s5(proj3, bblk, a_re, a_im, cblk, d_skip, w_glu):
    nb, seq, _ = proj3.shape
    lt = min(S5_LT, seq)
    kern = functools.partial(_s5_kernel, nb=nb, lt=lt)
    return pl.pallas_call(
        kern,
        grid=(seq // lt,),
        in_specs=[
            pl.BlockSpec((nb, lt, SSM_WIDTH), lambda c: (0, c, COL_SSM // SSM_WIDTH)),
            _const_spec((SSM_WIDTH, 2 * N_STATE)),
            _const_spec((1, N_STATE)),
            _const_spec((1, N_STATE)),
            _const_spec((2 * N_STATE, SSM_WIDTH)),
            _const_spec((1, SSM_WIDTH)),
            _const_spec((SSM_WIDTH, 2 * D_MODEL)),
        ],
        out_specs=pl.BlockSpec((nb, lt, D_MODEL), lambda c: (0, c, 0)),
        out_shape=jax.ShapeDtypeStruct((nb, seq, D_MODEL), F32),
        scratch_shapes=[pltpu.VMEM((2 * N_STATE // LANES, nb * lt, LANES), F32),
                        pltpu.VMEM((2 * N_STATE // LANES, nb, LANES), F32)],
        compiler_params=_cparams(("arbitrary",)),
        name="s5",
    )(proj3, bblk, a_re, a_im, cblk, d_skip, w_glu)


def _s5_params(lam_re, lam_im, log_dt, b_re, b_im, c_re, c_im):
    g, p, h = SSM_GROUPS, SSM_STATE, SSM_GROUP
    lam = lax.complex(lam_re.astype(F32), lam_im.astype(F32))
    dt = jnp.exp(log_dt.astype(F32))[:, None]
    a_bar = jnp.exp(lam * dt)
    b_bar = ((a_bar - 1.0) / lam)[..., None] * lax.complex(b_re.astype(F32), b_im.astype(F32))
    eye = jnp.eye(g, dtype=F32)
    bb_re = jnp.einsum("gph,gk->ghkp", jnp.real(b_bar), eye).reshape(g * h, g * p)
    bb_im = jnp.einsum("gph,gk->ghkp", jnp.imag(b_bar), eye).reshape(g * h, g * p)
    bblk = jnp.concatenate([bb_re, bb_im], axis=1).astype(BF16)
    cc_re = jnp.einsum("ghp,gk->kpgh", c_re.astype(F32), eye).reshape(g * p, g * h)
    cc_im = jnp.einsum("ghp,gk->kpgh", c_im.astype(F32), eye).reshape(g * p, g * h)
    cblk = jnp.concatenate([cc_re, -cc_im], axis=0).astype(BF16)
    return bblk, jnp.real(a_bar).reshape(1, g * p), jnp.imag(a_bar).reshape(1, g * p), cblk


def _mlstm_kernel(xm_ref, z_ref, gt_ref, cw_ref, cb_ref, wq_ref, wk_ref, wv_ref, gb_ref, gain_ref, skip_ref,
                  wp_ref, o_ref, cbuf_ref, c_ref, n_ref, m_ref, h_ref, *, tt):
    halo = SUBLANES

    @pl.when(pl.program_id(1) == 0)
    def _():
        cbuf_ref[0:halo, :] = jnp.zeros((halo, MLSTM_WIDTH), F32)
        c_ref[...] = jnp.zeros_like(c_ref)
        n_ref[...] = jnp.zeros_like(n_ref)
        m_ref[...] = jnp.zeros_like(m_ref)

    @pl.when(pl.program_id(1) > 0)
    def _():
        cbuf_ref[0:halo, :] = cbuf_ref[tt:tt + halo, :]

    xm = xm_ref[0]
    cbuf_ref[halo:halo + tt, :] = xm
    conv = cb_ref[...]
    for j in range(CONV_WIDTH):
        off = halo - (CONV_WIDTH - 1) + j
        conv = conv + cbuf_ref[off:off + tt, :] * cw_ref[j:j + 1, :]
    xc = conv * jax.nn.sigmoid(conv)
    xcb = xc.astype(BF16)
    xmb = xm.astype(BF16)

    row_i = lax.broadcasted_iota(jnp.int32, (CHUNK, CHUNK), 0)
    col_i = lax.broadcasted_iota(jnp.int32, (CHUNK, CHUNK), 1)
    causal = col_i <= row_i
    tril = causal.astype(F32)
    triu = (row_i <= col_i).astype(F32)

    qs, ks, vs = [], [], []
    for h in range(HEADS):
        hs = slice(h * HEAD_DIM, (h + 1) * HEAD_DIM)
        qs.append(jnp.dot(xcb[:, hs], wq_ref[h], preferred_element_type=F32))
        ks.append(jnp.dot(xcb[:, hs], wk_ref[h], preferred_element_type=F32) * (HEAD_DIM ** -0.5))
        vs.append(jnp.dot(xmb[:, hs], wv_ref[h], preferred_element_type=F32))

    for j in range(tt // CHUNK):
        rs = slice(j * CHUNK, (j + 1) * CHUNK)
        gcol = gt_ref[0, rs, :] + gb_ref[...]
        lf_col = jax.nn.log_sigmoid(gcol)
        bcum_col = jnp.dot(tril, lf_col, preferred_element_type=F32, precision=lax.Precision.HIGHEST)
        grow = gcol.T
        lf_row = jax.nn.log_sigmoid(grow[0:SUBLANES, :])
        bcum_row = jnp.dot(lf_row, triu, preferred_element_type=F32, precision=lax.Precision.HIGHEST)
        for h in range(HEADS):
            hs = slice(h * HEAD_DIM, (h + 1) * HEAD_DIM)
            qc, kc, vc = qs[h][rs], ks[h][rs], vs[h][rs]
            bc = bcum_col[:, HEADS + h:HEADS + h + 1]
            br = bcum_row[HEADS + h:HEADS + h + 1, :]
            ic = gcol[:, h:h + 1]
            ir = grow[h:h + 1, :]
            m_prev = m_ref[h:h + 1, 0:1]
            log_w = jnp.where(causal, bc - br + ir, -jnp.inf)
            log_inter = bc + m_prev
            m_t = jnp.maximum(log_inter, jnp.max(log_w, axis=-1, keepdims=True))
            w = jnp.exp(log_w - m_t)
            inter = jnp.exp(log_inter - m_t)
            qcb, kcb, vcb = qc.astype(BF16), kc.astype(BF16), vc.astype(BF16)
            s = lax.dot_general(qcb, kcb, (((1,), (1,)), ((), ())), preferred_element_type=F32)
            sw = s * w
            c_prev = c_ref[h]
            n_prev = n_ref[h:h + 1, :]
            qcmem = lax.dot_general(qcb, c_prev.astype(BF16), (((1,), (1,)), ((), ())),
                                    preferred_element_type=F32)
            num = jnp.dot(sw.astype(BF16), vcb, preferred_element_type=F32) + inter * qcmem
            den = jnp.sum(sw, axis=-1, keepdims=True) + inter * jnp.sum(qc * n_prev, axis=-1, keepdims=True)
            hh = num / jnp.maximum(jnp.abs(den), jnp.exp(-m_t))
            h_ref[rs, hs] = hh
            b_last = bc[CHUNK - 1:CHUNK, :]
            m_new = m_t[CHUNK - 1:CHUNK, :]
            w_end = jnp.exp(b_last - bc + ic - m_new)
            decay = jnp.exp(b_last + m_prev - m_new)
            vw = (vc * w_end).astype(BF16)
            c_ref[h] = decay * c_prev + lax.dot_general(vw, kcb, (((0,), (0,)), ((), ())),
                                                        preferred_element_type=F32)
            n_ref[h:h + 1, :] = decay * n_prev + jnp.sum(w_end * kc, axis=0, keepdims=True)
            m_ref[h:h + 1, :] = jnp.broadcast_to(m_new, (1, LANES))

    z = z_ref[0]
    outs = []
    for h in range(HEADS):
        hs = slice(h * HEAD_DIM, (h + 1) * HEAD_DIM)
        hg = jax.nn.sigmoid(z[:, hs]) * h_ref[:, hs]
        mu = jnp.mean(hg, axis=-1, keepdims=True)
        dv = hg - mu
        var = jnp.mean(dv * dv, axis=-1, keepdims=True)
        outs.append(dv * lax.rsqrt(var + EPS))
    hn = jnp.concatenate(outs, axis=-1)
    hn = hn * gain_ref[...] + skip_ref[...] * xc
    o_ref[0] = jnp.dot(hn.astype(BF16), wp_ref[...], preferred_element_type=F32)


def _mlstm(proj3, conv_w, conv_b, w_q, w_k, w_v, gate_bias, gain, skip, w_proj):
    nb, seq, _ = proj3.shape
    tt = min(MLSTM_TT, seq)
    kern = functools.partial(_mlstm_kernel, tt=tt)
    return pl.pallas_call(
        kern,
        grid=(nb, seq // tt),
        in_specs=[
            pl.BlockSpec((1, tt, MLSTM_WIDTH), lambda b, t: (b, t, COL_XM // MLSTM_WIDTH)),
            pl.BlockSpec((1, tt, MLSTM_WIDTH), lambda b, t: (b, t, COL_Z // MLSTM_WIDTH)),
            pl.BlockSpec((1, tt, LANES), lambda b, t: (b, t, COL_GATE // LANES)),
            _const_spec((CONV_WIDTH, MLSTM_WIDTH)),
            _const_spec((1, MLSTM_WIDTH)),
            _const_spec((HEADS, HEAD_DIM, HEAD_DIM)),
            _const_spec((HEADS, HEAD_DIM, HEAD_DIM)),
            _const_spec((HEADS, HEAD_DIM, HEAD_DIM)),
            _const_spec((1, LANES)),
            _const_spec((1, MLSTM_WIDTH)),
            _const_spec((1, MLSTM_WIDTH)),
            _const_spec((MLSTM_WIDTH, D_MODEL)),
        ],
        out_specs=pl.BlockSpec((1, tt, D_MODEL), lambda b, t: (b, t, 0)),
        out_shape=jax.ShapeDtypeStruct((nb, seq, D_MODEL), F32),
        scratch_shapes=[
            pltpu.VMEM((tt + SUBLANES, MLSTM_WIDTH), F32),
            pltpu.VMEM((HEADS, HEAD_DIM, HEAD_DIM), F32),
            pltpu.VMEM((SUBLANES, HEAD_DIM), F32),
            pltpu.VMEM((SUBLANES, LANES), F32),
            pltpu.VMEM((tt, MLSTM_WIDTH), F32),
        ],
        compiler_params=_cparams(("arbitrary", "arbitrary")),
        name="mlstm",
    )(proj3, proj3, proj3, conv_w, conv_b, w_q, w_k, w_v, gate_bias, gain, skip, w_proj)


def _merge_kernel(ga_ref, gb_ref, ya_ref, yb_ref, x_ref, wo_ref, g2_ref, wq_ref, h1_ref, hn_ref, q_ref):
    merged = jax.nn.sigmoid(ga_ref[...]) * ya_ref[...] + jax.nn.sigmoid(gb_ref[...]) * yb_ref[...]
    h1 = x_ref[...] + jnp.dot(merged.astype(BF16), wo_ref[...], preferred_element_type=F32)
    h1_ref[...] = h1
    hn = h1 * lax.rsqrt(jnp.mean(h1 * h1, axis=-1, keepdims=True) + EPS) * g2_ref[...]
    hn_ref[...] = hn
    q_ref[...] = jnp.dot(hn.astype(BF16), wq_ref[...], preferred_element_type=F32)


def _merge(proj, y_a, y_b, x2, w_out, g2, w_query):
    t = x2.shape[0]
    tm = min(MERGE_TM, t)
    row = lambda c: pl.BlockSpec((tm, D_MODEL), lambda i: (i, c))
    out_sd = jax.ShapeDtypeStruct((t, D_MODEL), F32)
    return pl.pallas_call(
        _merge_kernel,
        grid=(t // tm,),
        in_specs=[row(COL_GA // D_MODEL), row(COL_GB // D_MODEL), row(0), row(0), row(0),
                  _const_spec((D_MODEL, D_MODEL)), _const_spec((1, D_MODEL)), _const_spec((D_MODEL, D_MODEL))],
        out_specs=[row(0), row(0), row(0)],
        out_shape=[out_sd, out_sd, out_sd],
        compiler_params=_cparams(("arbitrary",)),
        name="merge",
    )(proj, proj, y_a, y_b, x2, w_out, g2, w_query)


_CAND_ROWS = [PEER_TOPK // (i + 1) for i in range(PEER_TOPK)]
_CAND_OFFS = [int(v) for v in np.cumsum([0] + _CAND_ROWS[:-1])]
_NCAND = int(sum(_CAND_ROWS))
_NCAND_PAD = 56


def _extract_topk(s, payload, k, val_ref, pay_ref):
    n = s.shape[0]
    big = jnp.int32(2 ** 30)
    for r in range(k):
        m = jnp.max(s, axis=0, keepdims=True)
        sel = jnp.min(jnp.where(s == m, payload, big), axis=0, keepdims=True)
        val_ref[r:r + 1, :] = m
        pay_ref[r:r + 1, :] = sel
        s = jnp.where(payload == sel, -jnp.inf, s)


def _route_kernel(q_ref, k1_ref, k2_ref, e_ref, g_ref, rec_ref, cnt_ref,
                  v1_ref, i1_ref, v2_ref, i2_ref, cv_ref, ce_ref, cp_ref, tv_ref, tp_ref, ea_ref, gate_ref,
                  rows_ref, lhs_ref, *, tb):
    key_iota = lax.broadcasted_iota(jnp.int32, (PEER_NKEYS, tb), 0)
    cand_iota = lax.broadcasted_iota(jnp.int32, (_NCAND_PAD, tb), 0)
    for h in range(PEER_HEADS):
        q1 = q_ref[:, h * 2 * PEER_HALF:h * 2 * PEER_HALF + PEER_HALF].astype(BF16)
        q2 = q_ref[:, h * 2 * PEER_HALF + PEER_HALF:(h + 1) * 2 * PEER_HALF].astype(BF16)
        s1 = lax.dot_general(k1_ref[h], q1, (((1,), (1,)), ((), ())), preferred_element_type=F32)
        s2 = lax.dot_general(k2_ref[h], q2, (((1,), (1,)), ((), ())), preferred_element_type=F32)
        _extract_topk(s1, key_iota, PEER_TOPK, v1_ref, i1_ref)
        _extract_topk(s2, key_iota, PEER_TOPK, v2_ref, i2_ref)
        cv_ref[...] = jnp.full((_NCAND_PAD, tb), -jnp.inf, F32)
        ce_ref[...] = jnp.zeros((_NCAND_PAD, tb), jnp.int32)
        cp_ref[...] = cand_iota + jnp.int32(1 << 20)
        for i in range(PEER_TOPK):
            n_i, off = _CAND_ROWS[i], _CAND_OFFS[i]
            cv_ref[off:off + n_i, :] = v1_ref[i:i + 1, :] + v2_ref[0:n_i, :]
            ce_ref[off:off + n_i, :] = i1_ref[i:i + 1, :] * PEER_NKEYS + i2_ref[0:n_i, :]
            cp_ref[off:off + n_i, :] = i * PEER_TOPK + lax.broadcasted_iota(jnp.int32, (n_i, tb), 0)
        cv = cv_ref[...]
        cp = cp_ref[...]
        _extract_topk(cv, cp, PEER_TOPK, tv_ref, tp_ref)
        ce = ce_ref[...]
        tv = tv_ref[...]
        ex = jnp.exp(tv - tv[0:1, :])
        gate_ref[h * PEER_TOPK:(h + 1) * PEER_TOPK, :] = ex / jnp.sum(ex, axis=0, keepdims=True)
        for r in range(PEER_TOPK):
            ea_ref[h * PEER_TOPK + r:h * PEER_TOPK + r + 1, :] = jnp.sum(
                jnp.where(cp == tp_ref[r:r + 1, :], ce, 0), axis=0, keepdims=True)

    e_all = ea_ref[...]
    g_all = gate_ref[...]
    chunk = e_all >> PEER_SHIFT
    e_loc = e_all & (PEER_EC - 1)
    ri = lax.broadcasted_iota(jnp.int32, (HK, HK), 0)
    ci = lax.broadcasted_iota(jnp.int32, (HK, HK), 1)
    lower = (ci < ri).astype(BF16)
    dest = jnp.zeros((HK, tb), F32)
    gstart = jnp.zeros((1, tb), F32)
    rows_ref[...] = jnp.zeros(rows_ref.shape, F32)
    for c in range(PEER_NCHUNK):
        mask = chunk == c
        maskf = mask.astype(F32)
        rank = jnp.dot(lower, mask.astype(BF16), preferred_element_type=F32)
        dest = dest + maskf * (gstart * PEER_G + rank)
        cnt = jnp.sum(maskf, axis=0, keepdims=True).astype(jnp.int32)
        ng = ((cnt + (PEER_G - 1)) >> 3).astype(F32)
        rows_ref[c:c + 1, :] = ng
        rows_ref[SUBLANES + c:SUBLANES + c + 1, :] = gstart
        gstart = gstart + ng
    dest_i = dest.astype(jnp.int32)
    for k in range(PEER_SLOTS):
        sel = dest_i == k
        e_ref[k:k + 1, :] = jnp.sum(jnp.where(sel, e_loc, 0), axis=0, keepdims=True)
        g_ref[k:k + 1, :] = jnp.sum(jnp.where(sel, g_all, 0.0), axis=0, keepdims=True)

    ng8 = rows_ref[0:SUBLANES, :]
    ti = lax.broadcasted_iota(jnp.int32, (tb, tb), 0)
    tj = lax.broadcasted_iota(jnp.int32, (tb, tb), 1)
    before = (ti < tj).astype(BF16)
    off8 = jnp.dot(ng8.astype(BF16), before, preferred_element_type=F32)
    rows_ref[2 * SUBLANES:3 * SUBLANES, :] = off8
    cnt_ref[...] = jnp.broadcast_to(jnp.sum(ng8, axis=1, keepdims=True), (SUBLANES, LANES)).astype(jnp.int32)
    cols = rows_ref[...].T
    tok_row = lax.broadcasted_iota(jnp.int32, (1, tb), 1).astype(F32)
    smax = tb * (HK // PEER_G)
    rec_ref[...] = jnp.zeros(rec_ref.shape, jnp.int32)
    for c in range(PEER_NCHUNK):
        off_row = off8[c:c + 1, :]
        off_hi = jnp.floor(off_row * (1.0 / 64.0))
        lhs_ref[...] = jnp.zeros(lhs_ref.shape, F32)
        lhs_ref[0:1, :] = tok_row
        lhs_ref[1:2, :] = rows_ref[SUBLANES + c:SUBLANES + c + 1, :]
        lhs_ref[2:3, :] = off_hi
        lhs_ref[3:4, :] = off_row - 64.0 * off_hi
        lhs = lhs_ref[...].astype(BF16)
        ng_col = cols[:, c:c + 1]
        off_col = cols[:, 2 * SUBLANES + c:2 * SUBLANES + c + 1]
        for j in range(smax // ROUTE_ST):
            s = (lax.broadcasted_iota(jnp.int32, (1, ROUTE_ST), 1) + j * ROUTE_ST).astype(F32)
            onehot = jnp.logical_and(off_col <= s, s < off_col + ng_col).astype(BF16)
            r = jnp.dot(lhs, onehot, preferred_element_type=F32)
            tok = r[0:1, :]
            slot = tok * PEER_SLOTS + (r[1:2, :] + s - (r[2:3, :] * 64.0 + r[3:4, :])) * PEER_G
            rec = tok.astype(jnp.int32) * 65536 + slot.astype(jnp.int32)
            rec_ref[c:c + 1, j * ROUTE_ST:(j + 1) * ROUTE_ST] = rec


def _route(qr, key1, key2):
    t = qr.shape[0]
    tb = min(PEER_TB, t)
    nblk = t // tb
    smax = tb * (HK // PEER_G)
    kern = functools.partial(_route_kernel, tb=tb)
    col = lambda r: pl.BlockSpec((r, tb), lambda i: (0, i))
    f_s = lambda r: pltpu.VMEM((r, tb), F32)
    i_s = lambda r: pltpu.VMEM((r, tb), jnp.int32)
    return pl.pallas_call(
        kern,
        grid=(nblk,),
        in_specs=[pl.BlockSpec((tb, D_MODEL), lambda i: (i, 0)),
                  _const_spec((PEER_HEADS, PEER_NKEYS, PEER_HALF)),
                  _const_spec((PEER_HEADS, PEER_NKEYS, PEER_HALF))],
        out_specs=[col(PEER_SLOTS), col(PEER_SLOTS),
                   pl.BlockSpec((SUBLANES, smax), lambda i: (i, 0)),
                   pl.BlockSpec((SUBLANES, LANES), lambda i: (i, 0))],
        out_shape=[jax.ShapeDtypeStruct((PEER_SLOTS, t), jnp.int32), jax.ShapeDtypeStruct((PEER_SLOTS, t), F32),
                   jax.ShapeDtypeStruct((nblk * SUBLANES, smax), jnp.int32),
                   jax.ShapeDtypeStruct((nblk * SUBLANES, LANES), jnp.int32)],
        scratch_shapes=[f_s(PEER_TOPK), i_s(PEER_TOPK), f_s(PEER_TOPK), i_s(PEER_TOPK),
                        f_s(_NCAND_PAD), i_s(_NCAND_PAD), i_s(_NCAND_PAD),
                        f_s(PEER_TOPK), i_s(PEER_TOPK), i_s(HK), f_s(HK),
                        f_s(LANES), f_s(SUBLANES)],
        compiler_params=_cparams(("arbitrary",)),
        name="route",
    )(qr, key1, key2)


def _peer_kernel(c_ref, cnt_ref, e_hbm, g_hbm, rec_hbm, t_ref, u_ref, v_ref, acc_ref, o_ref,
                 e_sm, g_sm, rec_sm, sems, a_ref, *p_refs, tb):
    c = c_ref[0]
    b = pl.program_id(0)
    slots = tb * PEER_SLOTS
    cp_e = pltpu.make_async_copy(e_hbm.at[pl.ds(b * slots, slots)], e_sm, sems.at[0])
    cp_g = pltpu.make_async_copy(g_hbm.at[pl.ds(b * slots, slots)], g_sm, sems.at[1])
    cp_r = pltpu.make_async_copy(rec_hbm.at[b * SUBLANES + c], rec_sm, sems.at[2])
    cp_e.start()
    cp_g.start()
    cp_r.start()
    o_ref[...] = acc_ref[...]
    cp_e.wait()
    cp_g.wait()
    cp_r.wait()
    n = cnt_ref[b * PEER_NCHUNK + c]

    def tree_sum(xs):
        while len(xs) > 1:
            xs = [xs[i] + xs[i + 1] for i in range(0, len(xs) - 1, 2)] + ([xs[-1]] if len(xs) % 2 else [])
        return xs[0]

    def groups(s0, width, carry):
        prev_tok, acc = carry
        toks, slots = [], []
        for j in range(width):
            rec = rec_sm[s0 + j]
            toks.append(rec >> 16)
            slots.append(rec & 0xFFFF)
        for j in range(width):
            tvec = t_ref[toks[j]]
            for k in range(PEER_G):
                prod = tvec * u_ref[e_sm[slots[j] + k]].astype(F32)
                p_refs[j][k * SUBLANES:(k + 1) * SUBLANES, :] = prod[0:SUBLANES] + prod[SUBLANES:]
        rs = [tree_sum([p_refs[j][pl.ds(i, PEER_G, stride=SUBLANES), :] for i in range(SUBLANES)])
              for j in range(width)]
        score = jnp.sum(jnp.concatenate(rs, axis=0), axis=-1, keepdims=True)
        a_ref[0:width * PEER_G, :] = jnp.broadcast_to(jax.nn.gelu(score), (width * PEER_G, LANES))
        for j in range(width):
            terms = []
            for k in range(PEER_G):
                row = j * PEER_G + k
                a = a_ref[row:row + 1, :] * g_sm[slots[j] + k]
                terms.append(a * v_ref[e_sm[slots[j] + k]].astype(F32))
            acc = jnp.where(toks[j] != prev_tok, acc_ref[toks[j]], acc) + tree_sum(terms)
            o_ref[toks[j]] = acc
            prev_tok = toks[j]
        return prev_tok, acc

    carry = (jnp.int32(-1), jnp.zeros((ROW_TILES, LANES), F32))
    n_main = n // PEER_UNROLL
    carry = lax.fori_loop(0, n_main, lambda i, cr: groups(i * PEER_UNROLL, PEER_UNROLL, cr), carry)
    lax.fori_loop(n_main * PEER_UNROLL, n, lambda s, cr: groups(s, 1, cr), carry)


def _peer(e_flat, g_flat, rec, cnt, t3, u3, v3, acc3):
    t = t3.shape[0]
    tb = min(PEER_TB, t)
    kern = functools.partial(_peer_kernel, tb=tb)
    tok_spec = pl.BlockSpec((tb, ROW_TILES, LANES), lambda b, c, n: (b, 0, 0))
    tab_spec = pl.BlockSpec((PEER_EC, ROW_TILES, LANES), lambda b, c, n: (c[0], 0, 0),
                            pipeline_mode=pl.Buffered(1))
    call = pl.pallas_call(
        kern,
        grid_spec=pltpu.PrefetchScalarGridSpec(
            num_scalar_prefetch=2,
            grid=(t // tb,),
            in_specs=[pl.BlockSpec(memory_space=pl.ANY), pl.BlockSpec(memory_space=pl.ANY),
                      pl.BlockSpec(memory_space=pl.ANY), tok_spec, tab_spec, tab_spec, tok_spec],
            out_specs=tok_spec,
            scratch_shapes=[
                pltpu.SMEM((tb * PEER_SLOTS,), jnp.int32),
                pltpu.SMEM((tb * PEER_SLOTS,), F32),
                pltpu.SMEM((rec.shape[1],), jnp.int32),
                pltpu.SemaphoreType.DMA((3,)),
                pltpu.VMEM((PEER_UNROLL * PEER_G, LANES), F32),
            ] + [pltpu.VMEM((PEER_G * SUBLANES, LANES), F32)] * PEER_UNROLL,
        ),
        out_shape=jax.ShapeDtypeStruct(acc3.shape, F32),
        input_output_aliases={8: 0},
        compiler_params=_cparams(("arbitrary",)),
        name="peer",
    )
    acc = acc3
    for c in range(PEER_NCHUNK):
        acc = call(jnp.full((1,), c, jnp.int32), cnt, e_flat, g_flat, rec, t3, u3, v3, acc)
    return acc


def _final_kernel(a_ref, g_ref, o_ref):
    a = a_ref[...]
    ms = jnp.mean(a * a, axis=-1, keepdims=True)
    o_ref[...] = a * lax.rsqrt(ms + EPS) * g_ref[...]


def _final(acc2, g):
    t = acc2.shape[0]
    tm = min(FINAL_TM, t)
    return pl.pallas_call(
        _final_kernel,
        grid=(t // tm,),
        in_specs=[pl.BlockSpec((tm, D_MODEL), lambda i: (i, 0)), pl.BlockSpec((1, D_MODEL), lambda i: (0, 0))],
        out_specs=pl.BlockSpec((tm, D_MODEL), lambda i: (i, 0)),
        out_shape=jax.ShapeDtypeStruct((t, D_MODEL), F32),
        compiler_params=_cparams(("arbitrary",)),
        name="final",
    )(acc2, g)


def _layer(x, norm1_g, w_in, lam_re, lam_im, log_dt, b_re, b_im, c_re, c_im, d_skip, w_glu, conv_w, conv_b,
           w_q, w_k, w_v, b_i, b_f, mh_gain, mlstm_skip, w_mlstm_out, w_out, norm2_g, w_query, key1, key2,
           expert_u, expert_v):
    nb, seq, d = x.shape
    t = nb * seq
    x2 = x.reshape(t, d)
    row = lambda a: a.reshape(1, -1).astype(F32)

    o_ssm, o_xm, o_z, o_i, o_f, o_ga, o_gb = 0, 512, 2048, 3584, 3588, 3592, 5640
    zeros = lambda n: jnp.zeros((d, n), w_in.dtype)
    w_cat = jnp.concatenate([
        w_in[:, o_xm:o_z], w_in[:, o_z:o_i], w_in[:, o_ssm:o_xm], w_in[:, o_i:o_ga],
        zeros(COL_GA - COL_GATE - 2 * HEADS), w_in[:, o_ga:o_gb], w_in[:, o_gb:]], axis=1).astype(BF16)
    proj = _inproj(x2, row(norm1_g), w_cat)
    proj3 = proj.reshape(nb, seq, PROJ_W)

    bblk, a_re, a_im, cblk = _s5_params(lam_re, lam_im, log_dt, b_re, b_im, c_re, c_im)
    y_a = _s5(proj3, bblk, a_re, a_im, cblk, row(d_skip), w_glu.astype(BF16))

    gate_bias = jnp.concatenate([b_i.astype(F32), b_f.astype(F32), jnp.zeros((LANES - 2 * HEADS,), F32)]).reshape(1, LANES)
    y_b = _mlstm(proj3, conv_w.astype(F32), row(conv_b), w_q.astype(BF16), w_k.astype(BF16), w_v.astype(BF16),
                 gate_bias, row(mh_gain), row(mlstm_skip), w_mlstm_out.astype(BF16))

    h1, hn2, qr = _merge(proj, y_a.reshape(t, d), y_b.reshape(t, d), x2, w_out.astype(BF16), row(norm2_g),
                         w_query.astype(BF16))

    e_t, g_t, rec, cnt8 = _route(qr, key1.astype(BF16), key2.astype(BF16))
    e_flat = e_t.T.reshape(t * PEER_SLOTS)
    g_flat = g_t.T.reshape(t * PEER_SLOTS)
    cnt = cnt8[:, 0].reshape(-1, SUBLANES)[:, :PEER_NCHUNK].reshape(-1)

    dense = lambda a: a.reshape(a.shape[0], ROW_TILES, LANES)
    acc = _peer(e_flat, g_flat, rec, cnt, dense(hn2), dense(expert_u.astype(BF16)), dense(expert_v.astype(BF16)),
                dense(h1))
    return acc.reshape(nb, seq, d)


def kernel(x, norm1_g, w_in, lam_re, lam_im, log_dt, b_re, b_im, c_re, c_im, d_skip, w_glu, conv_w, conv_b, w_q, w_k, w_v, b_i, b_f, mh_gain, mlstm_skip, w_mlstm_out, w_out, norm2_g, w_query, key1, key2, expert_u, expert_v, norm_f_g):
    depth = w_in.shape[0]
    h = x
    for l in range(depth):
        h = _layer(h, norm1_g[l], w_in[l], lam_re[l], lam_im[l], log_dt[l], b_re[l], b_im[l], c_re[l], c_im[l],
                   d_skip[l], w_glu[l], conv_w[l], conv_b[l], w_q[l], w_k[l], w_v[l], b_i[l], b_f[l], mh_gain[l],
                   mlstm_skip[l], w_mlstm_out[l], w_out[l], norm2_g[l], w_query[l], key1[l], key2[l],
                   expert_u[l], expert_v[l])
    nb, seq, d = h.shape
    out = _final(h.reshape(nb * seq, d), norm_f_g.reshape(1, d).astype(F32))
    return out.reshape(nb, seq, d)
```

```python
import functools
import math

import jax
import jax.numpy as jnp
import numpy as np
from jax import lax
from jax.experimental import pallas as pl
from jax.experimental.pallas import tpu as pltpu

F32 = jnp.float32
BF16 = jnp.bfloat16
EPS = 1e-6

D_MODEL = 2048
SSM_WIDTH = 512
SSM_GROUP = 16
SSM_GROUPS = 32
SSM_STATE = 64
N_STATE = SSM_GROUPS * SSM_STATE
MLSTM_WIDTH = 1536
HEADS = 4
HEAD_DIM = 384
CONV_WIDTH = 4
CHUNK = 64
PEER_HEADS = 8
PEER_HALF = 128
PEER_NKEYS = 128
PEER_TOPK = 16
PEER_EXPERTS = PEER_NKEYS * PEER_NKEYS
HK = PEER_HEADS * PEER_TOPK

LANES = 128
SUBLANES = 8
ROW_TILES = D_MODEL // LANES

COL_XM = 0
COL_Z = 1536
COL_SSM = 3072
COL_GATE = 3584
COL_GA = 4096
COL_GB = 6144
PROJ_W = 8192

INPROJ_TM = 1024
INPROJ_TN = 1024
S5_LT = 32
MLSTM_TT = 256
MERGE_TM = 256
ROUTE_ST = 512
PEER_EC = 4096
PEER_NCHUNK = PEER_EXPERTS // PEER_EC
PEER_SHIFT = int(math.log2(PEER_EC))
PEER_TB = 256
PEER_G = 8
PEER_TGROUPS = 20
PEER_SLOTS = PEER_TGROUPS * PEER_G
PEER_UNROLL = 4
FINAL_TM = 512
assert PEER_G == SUBLANES and HK + PEER_NCHUNK * (PEER_G - 1) <= PEER_SLOTS

VMEM_LIMIT = 56 * 1024 * 1024


def _cparams(sem):
    return pltpu.CompilerParams(dimension_semantics=sem, vmem_limit_bytes=VMEM_LIMIT)


def _const_spec(shape):
    nd = len(shape)
    return pl.BlockSpec(shape, lambda *_: (0,) * nd, pipeline_mode=pl.Buffered(1))


def _inproj_kernel(x_ref, g_ref, w_ref, o_ref, hn_ref):
    @pl.when(pl.program_id(1) == 0)
    def _():
        x = x_ref[...]
        y = x * lax.rsqrt(jnp.mean(x * x, axis=-1, keepdims=True) + EPS) * g_ref[...]
        hn_ref[...] = y.astype(BF16)

    o_ref[...] = jnp.dot(hn_ref[...], w_ref[...], preferred_element_type=F32)


def _inproj(x2, g, w_cat):
    t = x2.shape[0]
    tm = min(INPROJ_TM, t)
    return pl.pallas_call(
        _inproj_kernel,
        grid=(t // tm, PROJ_W // INPROJ_TN),
        in_specs=[
            pl.BlockSpec((tm, D_MODEL), lambda i, n: (i, 0)),
            pl.BlockSpec((1, D_MODEL), lambda i, n: (0, 0)),
            pl.BlockSpec((D_MODEL, INPROJ_TN), lambda i, n: (0, n)),
        ],
        out_specs=pl.BlockSpec((tm, INPROJ_TN), lambda i, n: (i, n)),
        out_shape=jax.ShapeDtypeStruct((t, PROJ_W), F32),
        scratch_shapes=[pltpu.VMEM((tm, D_MODEL), BF16)],
        compiler_params=_cparams(("arbitrary", "arbitrary")),
        name="inproj",
    )(x2, g, w_cat)


def _s5_kernel(u_ref, bblk_ref, are_ref, aim_ref, cblk_ref, d_ref, wglu_ref, o_ref, x_ref, st_ref, *, nb, lt):
    @pl.when(pl.program_id(0) == 0)
    def _():
        st_ref[...] = jnp.zeros_like(st_ref)

    rows = nb * lt
    u = u_ref[...].reshape(rows, SSM_WIDTH)
    bu = jnp.dot(u.astype(BF16), bblk_ref[...], preferred_element_type=F32)
    n_tiles = N_STATE // LANES
    for j in range(2 * n_tiles):
        x_ref[j] = bu[:, j * LANES:(j + 1) * LANES]

    tiles_per_pass = 4
    for t0 in range(0, n_tiles, tiles_per_pass):
        tiles = range(t0, t0 + tiles_per_pass)
        a_re = [are_ref[:, j * LANES:(j + 1) * LANES] for j in tiles]
        a_im = [aim_ref[:, j * LANES:(j + 1) * LANES] for j in tiles]

        def step(s, carry):
            r = pl.ds(s, nb, stride=lt)
            out = []
            for i, j in enumerate(tiles):
                s_re, s_im = carry[2 * i], carry[2 * i + 1]
                n_re = a_re[i] * s_re - a_im[i] * s_im + x_ref[j, r, :]
                n_im = a_re[i] * s_im + a_im[i] * s_re + x_ref[n_tiles + j, r, :]
                x_ref[j, r, :] = n_re
                x_ref[n_tiles + j, r, :] = n_im
                out += [n_re, n_im]
            return tuple(out)

        init = []
        for j in tiles:
            init += [st_ref[j], st_ref[n_tiles + j]]
        fin = lax.fori_loop(0, lt, step, tuple(init))
        for i, j in enumerate(tiles):
            st_ref[j] = fin[2 * i]
            st_ref[n_tiles + j] = fin[2 * i + 1]

    xs = jnp.concatenate([x_ref[j].astype(BF16) for j in range(2 * n_tiles)], axis=-1)
    y = jnp.dot(xs, cblk_ref[...], preferred_element_type=F32) + d_ref[...] * u
    y = jax.nn.gelu(y)
    vg = jnp.dot(y.astype(BF16), wglu_ref[...], preferred_element_type=F32)
    out = vg[:, :D_MODEL] * jax.nn.sigmoid(vg[:, D_MODEL:])
    o_ref[...] = out.reshape(nb, lt, D_MODEL)


def _s5(proj3, bblk, a_re, a_im, cblk, d_skip, w_glu):
    nb, seq, _ = proj3.shape
    lt = min(S5_LT, seq)
    kern = functools.partial(_s5_kernel, nb=nb, lt=lt)
    return pl.pallas_call(
        kern,
        grid=(seq // lt,),
        in_specs=[
            pl.BlockSpec((nb, lt, SSM_WIDTH), lambda c: (0, c, COL_SSM // SSM_WIDTH)),
            _const_spec((SSM_WIDTH, 2 * N_STATE)),
            _const_spec((1, N_STATE)),
            _const_spec((1, N_STATE)),
            _const_spec((2 * N_STATE, SSM_WIDTH)),
            _const_spec((1, SSM_WIDTH)),
            _const_spec((SSM_WIDTH, 2 * D_MODEL)),
        ],
        out_specs=pl.BlockSpec((nb, lt, D_MODEL), lambda c: (0, c, 0)),
        out_shape=jax.ShapeDtypeStruct((nb, seq, D_MODEL), F32),
        scratch_shapes=[pltpu.VMEM((2 * N_STATE // LANES, nb * lt, LANES), F32),
                        pltpu.VMEM((2 * N_STATE // LANES, nb, LANES), F32)],
        compiler_params=_cparams(("arbitrary",)),
        name="s5",
    )(proj3, bblk, a_re, a_im, cblk, d_skip, w_glu)


def _s5_params(lam_re, lam_im, log_dt, b_re, b_im, c_re, c_im):
    g, p, h = SSM_GROUPS, SSM_STATE, SSM_GROUP
    lam = lax.complex(lam_re.astype(F32), lam_im.astype(F32))
    dt = jnp.exp(log_dt.astype(F32))[:, None]
    a_bar = jnp.exp(lam * dt)
    b_bar = ((a_bar - 1.0) / lam)[..., None] * lax.complex(b_re.astype(F32), b_im.astype(F32))
    eye = jnp.eye(g, dtype=F32)
    bb_re = jnp.einsum("gph,gk->ghkp", jnp.real(b_bar), eye).reshape(g * h, g * p)
    bb_im = jnp.einsum("gph,gk->ghkp", jnp.imag(b_bar), eye).reshape(g * h, g * p)
    bblk = jnp.concatenate([bb_re, bb_im], axis=1).astype(BF16)
    cc_re = jnp.einsum("ghp,gk->kpgh", c_re.astype(F32), eye).reshape(g * p, g * h)
    cc_im = jnp.einsum("ghp,gk->kpgh", c_im.astype(F32), eye).reshape(g * p, g * h)
    cblk = jnp.concatenate([cc_re, -cc_im], axis=0).astype(BF16)
    return bblk, jnp.real(a_bar).reshape(1, g * p), jnp.imag(a_bar).reshape(1, g * p), cblk


def _mlstm_kernel(xm_ref, z_ref, gt_ref, cw_ref, cb_ref, wq_ref, wk_ref, wv_ref, gb_ref, gain_ref, skip_ref,
                  wp_ref, o_ref, cbuf_ref, c_ref, n_ref, m_ref, h_ref, *, tt):
    halo = SUBLANES

    @pl.when(pl.program_id(1) == 0)
    def _():
        cbuf_ref[0:halo, :] = jnp.zeros((halo, MLSTM_WIDTH), F32)
        c_ref[...] = jnp.zeros_like(c_ref)
        n_ref[...] = jnp.zeros_like(n_ref)
        m_ref[...] = jnp.zeros_like(m_ref)

    @pl.when(pl.program_id(1) > 0)
    def _():
        cbuf_ref[0:halo, :] = cbuf_ref[tt:tt + halo, :]

    xm = xm_ref[0]
    cbuf_ref[halo:halo + tt, :] = xm
    conv = cb_ref[...]
    for j in range(CONV_WIDTH):
        off = halo - (CONV_WIDTH - 1) + j
        conv = conv + cbuf_ref[off:off + tt, :] * cw_ref[j:j + 1, :]
    xc = conv * jax.nn.sigmoid(conv)
    xcb = xc.astype(BF16)
    xmb = xm.astype(BF16)

    row_i = lax.broadcasted_iota(jnp.int32, (CHUNK, CHUNK), 0)
    col_i = lax.broadcasted_iota(jnp.int32, (CHUNK, CHUNK), 1)
    causal = col_i <= row_i
    tril = causal.astype(F32)
    triu = (row_i <= col_i).astype(F32)

    qs, ks, vs = [], [], []
    for h in range(HEADS):
        hs = slice(h * HEAD_DIM, (h + 1) * HEAD_DIM)
        qs.append(jnp.dot(xcb[:, hs], wq_ref[h], preferred_element_type=F32))
        ks.append(jnp.dot(xcb[:, hs], wk_ref[h], preferred_element_type=F32) * (HEAD_DIM ** -0.5))
        vs.append(jnp.dot(xmb[:, hs], wv_ref[h], preferred_element_type=F32))

    for j in range(tt // CHUNK):
        rs = slice(j * CHUNK, (j + 1) * CHUNK)
        gcol = gt_ref[0, rs, :] + gb_ref[...]
        lf_col = jax.nn.log_sigmoid(gcol)
        bcum_col = jnp.dot(tril, lf_col, preferred_element_type=F32, precision=lax.Precision.HIGHEST)
        grow = gcol.T
        lf_row = jax.nn.log_sigmoid(grow[0:SUBLANES, :])
        bcum_row = jnp.dot(lf_row, triu, preferred_element_type=F32, precision=lax.Precision.HIGHEST)
        for h in range(HEADS):
            hs = slice(h * HEAD_DIM, (h + 1) * HEAD_DIM)
            qc, kc, vc = qs[h][rs], ks[h][rs], vs[h][rs]
            bc = bcum_col[:, HEADS + h:HEADS + h + 1]
            br = bcum_row[HEADS + h:HEADS + h + 1, :]
            ic = gcol[:, h:h + 1]
            ir = grow[h:h + 1, :]
            m_prev = m_ref[h:h + 1, 0:1]
            log_w = jnp.where(causal, bc - br + ir, -jnp.inf)
            log_inter = bc + m_prev
            m_t = jnp.maximum(log_inter, jnp.max(log_w, axis=-1, keepdims=True))
            w = jnp.exp(log_w - m_t)
            inter = jnp.exp(log_inter - m_t)
            qcb, kcb, vcb = qc.astype(BF16), kc.astype(BF16), vc.astype(BF16)
            s = lax.dot_general(qcb, kcb, (((1,), (1,)), ((), ())), preferred_element_type=F32)
            sw = s * w
            c_prev = c_ref[h]
            n_prev = n_ref[h:h + 1, :]
            qcmem = lax.dot_general(qcb, c_prev.astype(BF16), (((1,), (1,)), ((), ())),
                                    preferred_element_type=F32)
            num = jnp.dot(sw.astype(BF16), vcb, preferred_element_type=F32) + inter * qcmem
            den = jnp.sum(sw, axis=-1, keepdims=True) + inter * jnp.sum(qc * n_prev, axis=-1, keepdims=True)
            hh = num / jnp.maximum(jnp.abs(den), jnp.exp(-m_t))
            h_ref[rs, hs] = hh
            b_last = bc[CHUNK - 1:CHUNK, :]
            m_new = m_t[CHUNK - 1:CHUNK, :]
            w_end = jnp.exp(b_last - bc + ic - m_new)
            decay = jnp.exp(b_last + m_prev - m_new)
            vw = (vc * w_end).astype(BF16)
            c_ref[h] = decay * c_prev + lax.dot_general(vw, kcb, (((0,), (0,)), ((), ())),
                                                        preferred_element_type=F32)
            n_ref[h:h + 1, :] = decay * n_prev + jnp.sum(w_end * kc, axis=0, keepdims=True)
            m_ref[h:h + 1, :] = jnp.broadcast_to(m_new, (1, LANES))

    z = z_ref[0]
    outs = []
    for h in range(HEADS):
        hs = slice(h * HEAD_DIM, (h + 1) * HEAD_DIM)
        hg = jax.nn.sigmoid(z[:, hs]) * h_ref[:, hs]
        mu = jnp.mean(hg, axis=-1, keepdims=True)
        dv = hg - mu
        var = jnp.mean(dv * dv, axis=-1, keepdims=True)
        outs.append(dv * lax.rsqrt(var + EPS))
    hn = jnp.concatenate(outs, axis=-1)
    hn = hn * gain_ref[...] + skip_ref[...] * xc
    o_ref[0] = jnp.dot(hn.astype(BF16), wp_ref[...], preferred_element_type=F32)


def _mlstm(proj3, conv_w, conv_b, w_q, w_k, w_v, gate_bias, gain, skip, w_proj):
    nb, seq, _ = proj3.shape
    tt = min(MLSTM_TT, seq)
    kern = functools.partial(_mlstm_kernel, tt=tt)
    return pl.pallas_call(
        kern,
        grid=(nb, seq // tt),
        in_specs=[
            pl.BlockSpec((1, tt, MLSTM_WIDTH), lambda b, t: (b, t, COL_XM // MLSTM_WIDTH)),
            pl.BlockSpec((1, tt, MLSTM_WIDTH), lambda b, t: (b, t, COL_Z // MLSTM_WIDTH)),
            pl.BlockSpec((1, tt, LANES), lambda b, t: (b, t, COL_GATE // LANES)),
            _const_spec((CONV_WIDTH, MLSTM_WIDTH)),
            _const_spec((1, MLSTM_WIDTH)),
            _const_spec((HEADS, HEAD_DIM, HEAD_DIM)),
            _const_spec((HEADS, HEAD_DIM, HEAD_DIM)),
            _const_spec((HEADS, HEAD_DIM, HEAD_DIM)),
            _const_spec((1, LANES)),
            _const_spec((1, MLSTM_WIDTH)),
            _const_spec((1, MLSTM_WIDTH)),
            _const_spec((MLSTM_WIDTH, D_MODEL)),
        ],
        out_specs=pl.BlockSpec((1, tt, D_MODEL), lambda b, t: (b, t, 0)),
        out_shape=jax.ShapeDtypeStruct((nb, seq, D_MODEL), F32),
        scratch_shapes=[
            pltpu.VMEM((tt + SUBLANES, MLSTM_WIDTH), F32),
            pltpu.VMEM((HEADS, HEAD_DIM, HEAD_DIM), F32),
            pltpu.VMEM((SUBLANES, HEAD_DIM), F32),
            pltpu.VMEM((SUBLANES, LANES), F32),
            pltpu.VMEM((tt, MLSTM_WIDTH), F32),
        ],
        compiler_params=_cparams(("arbitrary", "arbitrary")),
        name="mlstm",
    )(proj3, proj3, proj3, conv_w, conv_b, w_q, w_k, w_v, gate_bias, gain, skip, w_proj)


def _merge_kernel(ga_ref, gb_ref, ya_ref, yb_ref, x_ref, wo_ref, g2_ref, wq_ref, h1_ref, hn_ref, q_ref):
    merged = jax.nn.sigmoid(ga_ref[...]) * ya_ref[...] + jax.nn.sigmoid(gb_ref[...]) * yb_ref[...]
    h1 = x_ref[...] + jnp.dot(merged.astype(BF16), wo_ref[...], preferred_element_type=F32)
    h1_ref[...] = h1
    hn = h1 * lax.rsqrt(jnp.mean(h1 * h1, axis=-1, keepdims=True) + EPS) * g2_ref[...]
    hn_ref[...] = hn
    q_ref[...] = jnp.dot(hn.astype(BF16), wq_ref[...], preferred_element_type=F32)


def _merge(proj, y_a, y_b, x2, w_out, g2, w_query):
    t = x2.shape[0]
    tm = min(MERGE_TM, t)
    row = lambda c: pl.BlockSpec((tm, D_MODEL), lambda i: (i, c))
    out_sd = jax.ShapeDtypeStruct((t, D_MODEL), F32)
    return pl.pallas_call(
        _merge_kernel,
        grid=(t // tm,),
        in_specs=[row(COL_GA // D_MODEL), row(COL_GB // D_MODEL), row(0), row(0), row(0),
                  _const_spec((D_MODEL, D_MODEL)), _const_spec((1, D_MODEL)), _const_spec((D_MODEL, D_MODEL))],
        out_specs=[row(0), row(0), row(0)],
        out_shape=[out_sd, out_sd, out_sd],
        compiler_params=_cparams(("arbitrary",)),
        name="merge",
    )(proj, proj, y_a, y_b, x2, w_out, g2, w_query)


_CAND_ROWS = [PEER_TOPK // (i + 1) for i in range(PEER_TOPK)]
_CAND_OFFS = [int(v) for v in np.cumsum([0] + _CAND_ROWS[:-1])]
_NCAND = int(sum(_CAND_ROWS))
_NCAND_PAD = 56


def _extract_topk(s, payload, k, val_ref, pay_ref):
    n = s.shape[0]
    big = jnp.int32(2 ** 30)
    for r in range(k):
        m = jnp.max(s, axis=0, keepdims=True)
        sel = jnp.min(jnp.where(s == m, payload, big), axis=0, keepdims=True)
        val_ref[r:r + 1, :] = m
        pay_ref[r:r + 1, :] = sel
        s = jnp.where(payload == sel, -jnp.inf, s)


def _route_kernel(q_ref, k1_ref, k2_ref, e_ref, g_ref, rec_ref, cnt_ref,
                  v1_ref, i1_ref, v2_ref, i2_ref, cv_ref, ce_ref, cp_ref, tv_ref, tp_ref, ea_ref, gate_ref,
                  rows_ref, lhs_ref, *, tb):
    key_iota = lax.broadcasted_iota(jnp.int32, (PEER_NKEYS, tb), 0)
    cand_iota = lax.broadcasted_iota(jnp.int32, (_NCAND_PAD, tb), 0)
    for h in range(PEER_HEADS):
        q1 = q_ref[:, h * 2 * PEER_HALF:h * 2 * PEER_HALF + PEER_HALF].astype(BF16)
        q2 = q_ref[:, h * 2 * PEER_HALF + PEER_HALF:(h + 1) * 2 * PEER_HALF].astype(BF16)
        s1 = lax.dot_general(k1_ref[h], q1, (((1,), (1,)), ((), ())), preferred_element_type=F32)
        s2 = lax.dot_general(k2_ref[h], q2, (((1,), (1,)), ((), ())), preferred_element_type=F32)
        _extract_topk(s1, key_iota, PEER_TOPK, v1_ref, i1_ref)
        _extract_topk(s2, key_iota, PEER_TOPK, v2_ref, i2_ref)
        cv_ref[...] = jnp.full((_NCAND_PAD, tb), -jnp.inf, F32)
        ce_ref[...] = jnp.zeros((_NCAND_PAD, tb), jnp.int32)
        cp_ref[...] = cand_iota + jnp.int32(1 << 20)
        for i in range(PEER_TOPK):
            n_i, off = _CAND_ROWS[i], _CAND_OFFS[i]
            cv_ref[off:off + n_i, :] = v1_ref[i:i + 1, :] + v2_ref[0:n_i, :]
            ce_ref[off:off + n_i, :] = i1_ref[i:i + 1, :] * PEER_NKEYS + i2_ref[0:n_i, :]
            cp_ref[off:off + n_i, :] = i * PEER_TOPK + lax.broadcasted_iota(jnp.int32, (n_i, tb), 0)
        cv = cv_ref[...]
        cp = cp_ref[...]
        _extract_topk(cv, cp, PEER_TOPK, tv_ref, tp_ref)
        ce = ce_ref[...]
        tv = tv_ref[...]
        ex = jnp.exp(tv - tv[0:1, :])
        gate_ref[h * PEER_TOPK:(h + 1) * PEER_TOPK, :] = ex / jnp.sum(ex, axis=0, keepdims=True)
        for r in range(PEER_TOPK):
            ea_ref[h * PEER_TOPK + r:h * PEER_TOPK + r + 1, :] = jnp.sum(
                jnp.where(cp == tp_ref[r:r + 1, :], ce, 0), axis=0, keepdims=True)

    e_all = ea_ref[...]
    g_all = gate_ref[...]
    chunk = e_all >> PEER_SHIFT
    e_loc = e_all & (PEER_EC - 1)
    ri = lax.broadcasted_iota(jnp.int32, (HK, HK), 0)
    ci = lax.broadcasted_iota(jnp.int32, (HK, HK), 1)
    lower = (ci < ri).astype(BF16)
    dest = jnp.zeros((HK, tb), F32)
    gstart = jnp.zeros((1, tb), F32)
    rows_ref[...] = jnp.zeros(rows_ref.shape, F32)
    for c in range(PEER_NCHUNK):
        mask = chunk == c
        maskf = mask.astype(F32)
        rank = jnp.dot(lower, mask.astype(BF16), preferred_element_type=F32)
        dest = dest + maskf * (gstart * PEER_G + rank)
        cnt = jnp.sum(maskf, axis=0, keepdims=True).astype(jnp.int32)
        ng = ((cnt + (PEER_G - 1)) >> 3).astype(F32)
        rows_ref[c:c + 1, :] = ng
        rows_ref[SUBLANES + c:SUBLANES + c + 1, :] = gstart
        gstart = gstart + ng
    dest_i = dest.astype(jnp.int32)
    for k in range(PEER_SLOTS):
        sel = dest_i == k
        e_ref[k:k + 1, :] = jnp.sum(jnp.where(sel, e_loc, 0), axis=0, keepdims=True)
        g_ref[k:k + 1, :] = jnp.sum(jnp.where(sel, g_all, 0.0), axis=0, keepdims=True)

    ng8 = rows_ref[0:SUBLANES, :]
    ti = lax.broadcasted_iota(jnp.int32, (tb, tb), 0)
    tj = lax.broadcasted_iota(jnp.int32, (tb, tb), 1)
    before = (ti < tj).astype(BF16)
    off8 = jnp.dot(ng8.astype(BF16), before, preferred_element_type=F32)
    rows_ref[2 * SUBLANES:3 * SUBLANES, :] = off8
    cnt_ref[...] = jnp.broadcast_to(jnp.sum(ng8, axis=1, keepdims=True), (SUBLANES, LANES)).astype(jnp.int32)
    cols = rows_ref[...].T
    tok_row = lax.broadcasted_iota(jnp.int32, (1, tb), 1).astype(F32)
    smax = tb * (HK // PEER_G)
    rec_ref[...] = jnp.zeros(rec_ref.shape, jnp.int32)
    for c in range(PEER_NCHUNK):
        off_row = off8[c:c + 1, :]
        off_hi = jnp.floor(off_row * (1.0 / 64.0))
        lhs_ref[...] = jnp.zeros(lhs_ref.shape, F32)
        lhs_ref[0:1, :] = tok_row
        lhs_ref[1:2, :] = rows_ref[SUBLANES + c:SUBLANES + c + 1, :]
        lhs_ref[2:3, :] = off_hi
        lhs_ref[3:4, :] = off_row - 64.0 * off_hi
        lhs = lhs_ref[...].astype(BF16)
        ng_col = cols[:, c:c + 1]
        off_col = cols[:, 2 * SUBLANES + c:2 * SUBLANES + c + 1]
        for j in range(smax // ROUTE_ST):
            s = (lax.broadcasted_iota(jnp.int32, (1, ROUTE_ST), 1) + j * ROUTE_ST).astype(F32)
            onehot = jnp.logical_and(off_col <= s, s < off_col + ng_col).astype(BF16)
            r = jnp.dot(lhs, onehot, preferred_element_type=F32)
            tok = r[0:1, :]
            gslot = tok * PEER_TGROUPS + (r[1:2, :] + s - (r[2:3, :] * 64.0 + r[3:4, :]))
            rec = tok.astype(jnp.int32) * 65536 + gslot.astype(jnp.int32)
            rec_ref[c:c + 1, j * ROUTE_ST:(j + 1) * ROUTE_ST] = rec


def _route(qr, key1, key2):
    t = qr.shape[0]
    tb = min(PEER_TB, t)
    nblk = t // tb
    smax = tb * (HK // PEER_G)
    kern = functools.partial(_route_kernel, tb=tb)
    col = lambda r: pl.BlockSpec((r, tb), lambda i: (0, i))
    f_s = lambda r: pltpu.VMEM((r, tb), F32)
    i_s = lambda r: pltpu.VMEM((r, tb), jnp.int32)
    return pl.pallas_call(
        kern,
        grid=(nblk,),
        in_specs=[pl.BlockSpec((tb, D_MODEL), lambda i: (i, 0)),
                  _const_spec((PEER_HEADS, PEER_NKEYS, PEER_HALF)),
                  _const_spec((PEER_HEADS, PEER_NKEYS, PEER_HALF))],
        out_specs=[col(PEER_SLOTS), col(PEER_SLOTS),
                   pl.BlockSpec((SUBLANES, smax), lambda i: (i, 0)),
                   pl.BlockSpec((SUBLANES, LANES), lambda i: (i, 0))],
        out_shape=[jax.ShapeDtypeStruct((PEER_SLOTS, t), jnp.int32), jax.ShapeDtypeStruct((PEER_SLOTS, t), F32),
                   jax.ShapeDtypeStruct((nblk * SUBLANES, smax), jnp.int32),
                   jax.ShapeDtypeStruct((nblk * SUBLANES, LANES), jnp.int32)],
        scratch_shapes=[f_s(PEER_TOPK), i_s(PEER_TOPK), f_s(PEER_TOPK), i_s(PEER_TOPK),
                        f_s(_NCAND_PAD), i_s(_NCAND_PAD), i_s(_NCAND_PAD),
                        f_s(PEER_TOPK), i_s(PEER_TOPK), i_s(HK), f_s(HK),
                        f_s(LANES), f_s(SUBLANES)],
        compiler_params=_cparams(("arbitrary",)),
        name="route",
    )(qr, key1, key2)


def _peer_kernel(c_ref, cnt_ref, e_hbm, g_hbm, rec_hbm, t_ref, u_ref, v_ref, acc_ref, o_ref,
                 rec_sm, sems, a_ref, r_ref, *lists_and_partials, tb):
    c = c_ref[0]
    b = pl.program_id(0)
    ngs = tb * PEER_TGROUPS
    e_sms = lists_and_partials[0:PEER_G]
    g_sms = lists_and_partials[PEER_G:2 * PEER_G]
    p_refs = lists_and_partials[2 * PEER_G:]
    copies = [pltpu.make_async_copy(rec_hbm.at[b * SUBLANES + c], rec_sm, sems.at[0])]
    for k in range(PEER_G):
        copies.append(pltpu.make_async_copy(e_hbm.at[k, pl.ds(b * ngs, ngs)], e_sms[k], sems.at[1 + k]))
        copies.append(pltpu.make_async_copy(g_hbm.at[k, pl.ds(b * ngs, ngs)], g_sms[k], sems.at[1 + PEER_G + k]))
    for cp in copies:
        cp.start()
    o_ref[...] = acc_ref[...]
    for cp in copies:
        cp.wait()
    n = cnt_ref[b * PEER_NCHUNK + c]

    def tree_sum(xs):
        while len(xs) > 1:
            xs = [xs[i] + xs[i + 1] for i in range(0, len(xs) - 1, 2)] + ([xs[-1]] if len(xs) % 2 else [])
        return xs[0]

    def decode(s0, width):
        recs = [rec_sm[s0 + j] for j in range(width)]
        return [r >> 16 for r in recs], [r & 0xFFFF for r in recs]

    def scores(s0, width):
        toks, gss = decode(s0, width)
        for j in range(width):
            t_lo = t_ref[toks[j], 0:SUBLANES, :]
            t_hi = t_ref[toks[j], SUBLANES:, :]
            for k in range(PEER_G):
                u = u_ref[e_sms[k][gss[j]]].astype(F32)
                p_refs[j][k * SUBLANES:(k + 1) * SUBLANES, :] = t_lo * u[0:SUBLANES] + t_hi * u[SUBLANES:]
        rs = [tree_sum([p_refs[j][pl.ds(i, PEER_G, stride=SUBLANES), :] for i in range(SUBLANES)])
              for j in range(width)]
        return jnp.concatenate(rs, axis=0)

    def activate(partials):
        score = jnp.sum(partials, axis=-1, keepdims=True)
        return jnp.broadcast_to(jax.nn.gelu(score), partials.shape)

    def store_act(act):
        pairs = act.shape[0]
        for s in range(SUBLANES):
            a_ref[pl.ds(s, pairs, stride=SUBLANES), :] = act

    def update(s0, width, carry):
        prev_tok, acc_lo, acc_hi = carry
        toks, gss = decode(s0, width)
        for j in range(width):
            lo, hi = [], []
            for k in range(PEER_G):
                pair = j * PEER_G + k
                a = a_ref[pair * SUBLANES:(pair + 1) * SUBLANES, :] * g_sms[k][gss[j]]
                v = v_ref[e_sms[k][gss[j]]].astype(F32)
                lo.append(a * v[0:SUBLANES])
                hi.append(a * v[SUBLANES:])
            new = toks[j] != prev_tok
            acc_lo = jnp.where(new, acc_ref[toks[j], 0:SUBLANES, :], acc_lo) + tree_sum(lo)
            acc_hi = jnp.where(new, acc_ref[toks[j], SUBLANES:, :], acc_hi) + tree_sum(hi)
            o_ref[toks[j], 0:SUBLANES, :] = acc_lo
            o_ref[toks[j], SUBLANES:, :] = acc_hi
            prev_tok = toks[j]
        return prev_tok, acc_lo, acc_hi

    w = PEER_UNROLL
    n_main = n // w
    last = jnp.maximum(n_main - 1, 0)
    zero = jnp.zeros((SUBLANES, LANES), F32)
    carry = (jnp.int32(-1), zero, zero)
    r_ref[...] = scores(0, w)
    store_act(activate(r_ref[...]))
    r_ref[...] = scores(jnp.minimum(1, last) * w, w)

    def pipelined(i, cr):
        act = activate(r_ref[...])
        partials = scores(jnp.minimum(i, last) * w, w)
        cr = update((i - 2) * w, w, cr)
        store_act(act)
        r_ref[...] = partials
        return cr

    carry = lax.fori_loop(2, n_main + 2, pipelined, carry)

    def single(s, cr):
        store_act(activate(scores(s, 1)))
        return update(s, 1, cr)

    lax.fori_loop(n_main * w, n, single, carry)


def _peer(e_flat, g_flat, rec, cnt, t3, u3, v3, acc3):
    t = t3.shape[0]
    tb = min(PEER_TB, t)
    kern = functools.partial(_peer_kernel, tb=tb)
    tok_spec = pl.BlockSpec((tb, ROW_TILES, LANES), lambda b, c, n: (b, 0, 0))
    tab_spec = pl.BlockSpec((PEER_EC, ROW_TILES, LANES), lambda b, c, n: (c[0], 0, 0),
                            pipeline_mode=pl.Buffered(1))
    call = pl.pallas_call(
        kern,
        grid_spec=pltpu.PrefetchScalarGridSpec(
            num_scalar_prefetch=2,
            grid=(t // tb,),
            in_specs=[pl.BlockSpec(memory_space=pl.ANY), pl.BlockSpec(memory_space=pl.ANY),
                      pl.BlockSpec(memory_space=pl.ANY), tok_spec, tab_spec, tab_spec, tok_spec],
            out_specs=tok_spec,
            scratch_shapes=[
                pltpu.SMEM((rec.shape[1],), jnp.int32),
                pltpu.SemaphoreType.DMA((1 + 2 * PEER_G,)),
                pltpu.VMEM((PEER_UNROLL * PEER_G * SUBLANES, LANES), F32),
                pltpu.VMEM((PEER_UNROLL * PEER_G, LANES), F32),
            ] + [pltpu.SMEM((tb * PEER_TGROUPS,), jnp.int32)] * PEER_G
              + [pltpu.SMEM((tb * PEER_TGROUPS,), F32)] * PEER_G
              + [pltpu.VMEM((PEER_G * SUBLANES, LANES), F32)] * PEER_UNROLL,
        ),
        out_shape=jax.ShapeDtypeStruct(acc3.shape, F32),
        input_output_aliases={8: 0},
        compiler_params=_cparams(("arbitrary",)),
        name="peer",
    )
    acc = acc3
    for c in range(PEER_NCHUNK):
        acc = call(jnp.full((1,), c, jnp.int32), cnt, e_flat, g_flat, rec, t3, u3, v3, acc)
    return acc


def _final_kernel(a_ref, g_ref, o_ref):
    a = a_ref[...]
    ms = jnp.mean(a * a, axis=-1, keepdims=True)
    o_ref[...] = a * lax.rsqrt(ms + EPS) * g_ref[...]


def _final(acc2, g):
    t = acc2.shape[0]
    tm = min(FINAL_TM, t)
    return pl.pallas_call(
        _final_kernel,
        grid=(t // tm,),
        in_specs=[pl.BlockSpec((tm, D_MODEL), lambda i: (i, 0)), pl.BlockSpec((1, D_MODEL), lambda i: (0, 0))],
        out_specs=pl.BlockSpec((tm, D_MODEL), lambda i: (i, 0)),
        out_shape=jax.ShapeDtypeStruct((t, D_MODEL), F32),
        compiler_params=_cparams(("arbitrary",)),
        name="final",
    )(acc2, g)


def _layer(x, norm1_g, w_in, lam_re, lam_im, log_dt, b_re, b_im, c_re, c_im, d_skip, w_glu, conv_w, conv_b,
           w_q, w_k, w_v, b_i, b_f, mh_gain, mlstm_skip, w_mlstm_out, w_out, norm2_g, w_query, key1, key2,
           expert_u, expert_v):
    nb, seq, d = x.shape
    t = nb * seq
    x2 = x.reshape(t, d)
    row = lambda a: a.reshape(1, -1).astype(F32)

    o_ssm, o_xm, o_z, o_i, o_f, o_ga, o_gb = 0, 512, 2048, 3584, 3588, 3592, 5640
    zeros = lambda n: jnp.zeros((d, n), w_in.dtype)
    w_cat = jnp.concatenate([
        w_in[:, o_xm:o_z], w_in[:, o_z:o_i], w_in[:, o_ssm:o_xm], w_in[:, o_i:o_ga],
        zeros(COL_GA - COL_GATE - 2 * HEADS), w_in[:, o_ga:o_gb], w_in[:, o_gb:]], axis=1).astype(BF16)
    proj = _inproj(x2, row(norm1_g), w_cat)
    proj3 = proj.reshape(nb, seq, PROJ_W)

    bblk, a_re, a_im, cblk = _s5_params(lam_re, lam_im, log_dt, b_re, b_im, c_re, c_im)
    y_a = _s5(proj3, bblk, a_re, a_im, cblk, row(d_skip), w_glu.astype(BF16))

    gate_bias = jnp.concatenate([b_i.astype(F32), b_f.astype(F32), jnp.zeros((LANES - 2 * HEADS,), F32)]).reshape(1, LANES)
    y_b = _mlstm(proj3, conv_w.astype(F32), row(conv_b), w_q.astype(BF16), w_k.astype(BF16), w_v.astype(BF16),
                 gate_bias, row(mh_gain), row(mlstm_skip), w_mlstm_out.astype(BF16))

    h1, hn2, qr = _merge(proj, y_a.reshape(t, d), y_b.reshape(t, d), x2, w_out.astype(BF16), row(norm2_g),
                         w_query.astype(BF16))

    e_t, g_t, rec, cnt8 = _route(qr, key1.astype(BF16), key2.astype(BF16))
    by_pair = lambda a: a.reshape(PEER_TGROUPS, PEER_G, t).transpose(1, 2, 0).reshape(PEER_G, t * PEER_TGROUPS)
    e_flat = by_pair(e_t)
    g_flat = by_pair(g_t)
    cnt = cnt8[:, 0].reshape(-1, SUBLANES)[:, :PEER_NCHUNK].reshape(-1)

    dense = lambda a: a.reshape(a.shape[0], ROW_TILES, LANES)
    acc = _peer(e_flat, g_flat, rec, cnt, dense(hn2), dense(expert_u.astype(BF16)), dense(expert_v.astype(BF16)),
                dense(h1))
    return acc.reshape(nb, seq, d)


def kernel(x, norm1_g, w_in, lam_re, lam_im, log_dt, b_re, b_im, c_re, c_im, d_skip, w_glu, conv_w, conv_b, w_q, w_k, w_v, b_i, b_f, mh_gain, mlstm_skip, w_mlstm_out, w_out, norm2_g, w_query, key1, key2, expert_u, expert_v, norm_f_g):
    depth = w_in.shape[0]
    h = x
    for l in range(depth):
        h = _layer(h, norm1_g[l], w_in[l], lam_re[l], lam_im[l], log_dt[l], b_re[l], b_im[l], c_re[l], c_im[l],
                   d_skip[l], w_glu[l], conv_w[l], conv_b[l], w_q[l], w_k[l], w_v[l], b_i[l], b_f[l], mh_gain[l],
                   mlstm_skip[l], w_mlstm_out[l], w_out[l], norm2_g[l], w_query[l], key1[l], key2[l],
                   expert_u[l], expert_v[l])
    nb, seq, d = h.shape
    out = _final(h.reshape(nb * seq, d), norm_f_g.reshape(1, d).astype(F32))
    return out.reshape(nb, seq, d)
```

```python
import functools
import math

import jax
import jax.numpy as jnp
import numpy as np
from jax import lax
from jax.experimental import pallas as pl
from jax.experimental.pallas import tpu as pltpu

F32 = jnp.float32
BF16 = jnp.bfloat16
EPS = 1e-6

D_MODEL = 2048
SSM_WIDTH = 512
SSM_GROUP = 16
SSM_GROUPS = 32
SSM_STATE = 64
N_STATE = SSM_GROUPS * SSM_STATE
MLSTM_WIDTH = 1536
HEADS = 4
HEAD_DIM = 384
CONV_WIDTH = 4
CHUNK = 64
PEER_HEADS = 8
PEER_HALF = 128
PEER_NKEYS = 128
PEER_TOPK = 16
PEER_EXPERTS = PEER_NKEYS * PEER_NKEYS
HK = PEER_HEADS * PEER_TOPK

LANES = 128
SUBLANES = 8
ROW_TILES = D_MODEL // LANES

COL_XM = 0
COL_Z = 1536
COL_SSM = 3072
COL_GATE = 3584
COL_GA = 4096
COL_GB = 6144
PROJ_W = 8192

INPROJ_TM = 1024
INPROJ_TN = 1024
S5_LT = 32
MLSTM_TT = 256
MERGE_TM = 256
ROUTE_ST = 512
PEER_EC = 4096
PEER_NCHUNK = PEER_EXPERTS // PEER_EC
PEER_SHIFT = int(math.log2(PEER_EC))
PEER_TB = 256
PEER_G = 8
PEER_TGROUPS = 20
PEER_SLOTS = PEER_TGROUPS * PEER_G
PEER_UNROLL = 4
PEER_LIST_ALIGN = 2 * PEER_UNROLL
FINAL_TM = 512
assert PEER_G == SUBLANES and HK + PEER_NCHUNK * (PEER_G - 1) <= PEER_SLOTS

VMEM_LIMIT = 56 * 1024 * 1024


def _cparams(sem):
    return pltpu.CompilerParams(dimension_semantics=sem, vmem_limit_bytes=VMEM_LIMIT)


def _const_spec(shape):
    nd = len(shape)
    return pl.BlockSpec(shape, lambda *_: (0,) * nd, pipeline_mode=pl.Buffered(1))


def _inproj_kernel(x_ref, g_ref, w_ref, o_ref, hn_ref):
    @pl.when(pl.program_id(1) == 0)
    def _():
        x = x_ref[...]
        y = x * lax.rsqrt(jnp.mean(x * x, axis=-1, keepdims=True) + EPS) * g_ref[...]
        hn_ref[...] = y.astype(BF16)

    o_ref[...] = jnp.dot(hn_ref[...], w_ref[...], preferred_element_type=F32)


def _inproj(x2, g, w_cat):
    t = x2.shape[0]
    tm = min(INPROJ_TM, t)
    return pl.pallas_call(
        _inproj_kernel,
        grid=(t // tm, PROJ_W // INPROJ_TN),
        in_specs=[
            pl.BlockSpec((tm, D_MODEL), lambda i, n: (i, 0)),
            pl.BlockSpec((1, D_MODEL), lambda i, n: (0, 0)),
            pl.BlockSpec((D_MODEL, INPROJ_TN), lambda i, n: (0, n)),
        ],
        out_specs=pl.BlockSpec((tm, INPROJ_TN), lambda i, n: (i, n)),
        out_shape=jax.ShapeDtypeStruct((t, PROJ_W), F32),
        scratch_shapes=[pltpu.VMEM((tm, D_MODEL), BF16)],
        compiler_params=_cparams(("arbitrary", "arbitrary")),
        name="inproj",
    )(x2, g, w_cat)


def _s5_kernel(u_ref, bblk_ref, are_ref, aim_ref, cblk_ref, d_ref, wglu_ref, o_ref, x_ref, st_ref, *, nb, lt):
    @pl.when(pl.program_id(0) == 0)
    def _():
        st_ref[...] = jnp.zeros_like(st_ref)

    rows = nb * lt
    u = u_ref[...].reshape(rows, SSM_WIDTH)
    bu = jnp.dot(u.astype(BF16), bblk_ref[...], preferred_element_type=F32)
    n_tiles = N_STATE // LANES
    for j in range(2 * n_tiles):
        x_ref[j] = bu[:, j * LANES:(j + 1) * LANES]

    tiles_per_pass = 4
    for t0 in range(0, n_tiles, tiles_per_pass):
        tiles = range(t0, t0 + tiles_per_pass)
        a_re = [are_ref[:, j * LANES:(j + 1) * LANES] for j in tiles]
        a_im = [aim_ref[:, j * LANES:(j + 1) * LANES] for j in tiles]

        def step(s, carry):
            r = pl.ds(s, nb, stride=lt)
            out = []
            for i, j in enumerate(tiles):
                s_re, s_im = carry[2 * i], carry[2 * i + 1]
                n_re = a_re[i] * s_re - a_im[i] * s_im + x_ref[j, r, :]
                n_im = a_re[i] * s_im + a_im[i] * s_re + x_ref[n_tiles + j, r, :]
                x_ref[j, r, :] = n_re
                x_ref[n_tiles + j, r, :] = n_im
                out += [n_re, n_im]
            return tuple(out)

        init = []
        for j in tiles:
            init += [st_ref[j], st_ref[n_tiles + j]]
        fin = lax.fori_loop(0, lt, step, tuple(init))
        for i, j in enumerate(tiles):
            st_ref[j] = fin[2 * i]
            st_ref[n_tiles + j] = fin[2 * i + 1]

    xs = jnp.concatenate([x_ref[j].astype(BF16) for j in range(2 * n_tiles)], axis=-1)
    y = jnp.dot(xs, cblk_ref[...], preferred_element_type=F32) + d_ref[...] * u
    y = jax.nn.gelu(y)
    vg = jnp.dot(y.astype(BF16), wglu_ref[...], preferred_element_type=F32)
    out = vg[:, :D_MODEL] * jax.nn.sigmoid(vg[:, D_MODEL:])
    o_ref[...] = out.reshape(nb, lt, D_MODEL)


def _s5(proj3, bblk, a_re, a_im, cblk, d_skip, w_glu):
    nb, seq, _ = proj3.shape
    lt = min(S5_LT, seq)
    kern = functools.partial(_s5_kernel, nb=nb, lt=lt)
    return pl.pallas_call(
        kern,
        grid=(seq // lt,),
        in_specs=[
            pl.BlockSpec((nb, lt, SSM_WIDTH), lambda c: (0, c, COL_SSM // SSM_WIDTH)),
            _const_spec((SSM_WIDTH, 2 * N_STATE)),
            _const_spec((1, N_STATE)),
            _const_spec((1, N_STATE)),
            _const_spec((2 * N_STATE, SSM_WIDTH)),
            _const_spec((1, SSM_WIDTH)),
            _const_spec((SSM_WIDTH, 2 * D_MODEL)),
        ],
        out_specs=pl.BlockSpec((nb, lt, D_MODEL), lambda c: (0, c, 0)),
        out_shape=jax.ShapeDtypeStruct((nb, seq, D_MODEL), F32),
        scratch_shapes=[pltpu.VMEM((2 * N_STATE // LANES, nb * lt, LANES), F32),
                        pltpu.VMEM((2 * N_STATE // LANES, nb, LANES), F32)],
        compiler_params=_cparams(("arbitrary",)),
        name="s5",
    )(proj3, bblk, a_re, a_im, cblk, d_skip, w_glu)


def _s5_params(lam_re, lam_im, log_dt, b_re, b_im, c_re, c_im):
    g, p, h = SSM_GROUPS, SSM_STATE, SSM_GROUP
    lam = lax.complex(lam_re.astype(F32), lam_im.astype(F32))
    dt = jnp.exp(log_dt.astype(F32))[:, None]
    a_bar = jnp.exp(lam * dt)
    b_bar = ((a_bar - 1.0) / lam)[..., None] * lax.complex(b_re.astype(F32), b_im.astype(F32))
    eye = jnp.eye(g, dtype=F32)
    bb_re = jnp.einsum("gph,gk->ghkp", jnp.real(b_bar), eye).reshape(g * h, g * p)
    bb_im = jnp.einsum("gph,gk->ghkp", jnp.imag(b_bar), eye).reshape(g * h, g * p)
    bblk = jnp.concatenate([bb_re, bb_im], axis=1).astype(BF16)
    cc_re = jnp.einsum("ghp,gk->kpgh", c_re.astype(F32), eye).reshape(g * p, g * h)
    cc_im = jnp.einsum("ghp,gk->kpgh", c_im.astype(F32), eye).reshape(g * p, g * h)
    cblk = jnp.concatenate([cc_re, -cc_im], axis=0).astype(BF16)
    return bblk, jnp.real(a_bar).reshape(1, g * p), jnp.imag(a_bar).reshape(1, g * p), cblk


def _mlstm_kernel(xm_ref, z_ref, gt_ref, cw_ref, cb_ref, wq_ref, wk_ref, wv_ref, gb_ref, gain_ref, skip_ref,
                  wp_ref, o_ref, cbuf_ref, c_ref, n_ref, m_ref, h_ref, *, tt):
    halo = SUBLANES

    @pl.when(pl.program_id(1) == 0)
    def _():
        cbuf_ref[0:halo, :] = jnp.zeros((halo, MLSTM_WIDTH), F32)
        c_ref[...] = jnp.zeros_like(c_ref)
        n_ref[...] = jnp.zeros_like(n_ref)
        m_ref[...] = jnp.zeros_like(m_ref)

    @pl.when(pl.program_id(1) > 0)
    def _():
        cbuf_ref[0:halo, :] = cbuf_ref[tt:tt + halo, :]

    xm = xm_ref[0]
    cbuf_ref[halo:halo + tt, :] = xm
    conv = cb_ref[...]
    for j in range(CONV_WIDTH):
        off = halo - (CONV_WIDTH - 1) + j
        conv = conv + cbuf_ref[off:off + tt, :] * cw_ref[j:j + 1, :]
    xc = conv * jax.nn.sigmoid(conv)
    xcb = xc.astype(BF16)
    xmb = xm.astype(BF16)

    row_i = lax.broadcasted_iota(jnp.int32, (CHUNK, CHUNK), 0)
    col_i = lax.broadcasted_iota(jnp.int32, (CHUNK, CHUNK), 1)
    causal = col_i <= row_i
    tril = causal.astype(F32)
    triu = (row_i <= col_i).astype(F32)

    qs, ks, vs = [], [], []
    for h in range(HEADS):
        hs = slice(h * HEAD_DIM, (h + 1) * HEAD_DIM)
        qs.append(jnp.dot(xcb[:, hs], wq_ref[h], preferred_element_type=F32))
        ks.append(jnp.dot(xcb[:, hs], wk_ref[h], preferred_element_type=F32) * (HEAD_DIM ** -0.5))
        vs.append(jnp.dot(xmb[:, hs], wv_ref[h], preferred_element_type=F32))

    for j in range(tt // CHUNK):
        rs = slice(j * CHUNK, (j + 1) * CHUNK)
        gcol = gt_ref[0, rs, :] + gb_ref[...]
        lf_col = jax.nn.log_sigmoid(gcol)
        bcum_col = jnp.dot(tril, lf_col, preferred_element_type=F32, precision=lax.Precision.HIGHEST)
        grow = gcol.T
        lf_row = jax.nn.log_sigmoid(grow[0:SUBLANES, :])
        bcum_row = jnp.dot(lf_row, triu, preferred_element_type=F32, precision=lax.Precision.HIGHEST)
        for h in range(HEADS):
            hs = slice(h * HEAD_DIM, (h + 1) * HEAD_DIM)
            qc, kc, vc = qs[h][rs], ks[h][rs], vs[h][rs]
            bc = bcum_col[:, HEADS + h:HEADS + h + 1]
            br = bcum_row[HEADS + h:HEADS + h + 1, :]
            ic = gcol[:, h:h + 1]
            ir = grow[h:h + 1, :]
            m_prev = m_ref[h:h + 1, 0:1]
            log_w = jnp.where(causal, bc - br + ir, -jnp.inf)
            log_inter = bc + m_prev
            m_t = jnp.maximum(log_inter, jnp.max(log_w, axis=-1, keepdims=True))
            w = jnp.exp(log_w - m_t)
            inter = jnp.exp(log_inter - m_t)
            qcb, kcb, vcb = qc.astype(BF16), kc.astype(BF16), vc.astype(BF16)
            s = lax.dot_general(qcb, kcb, (((1,), (1,)), ((), ())), preferred_element_type=F32)
            sw = s * w
            c_prev = c_ref[h]
            n_prev = n_ref[h:h + 1, :]
            qcmem = lax.dot_general(qcb, c_prev.astype(BF16), (((1,), (1,)), ((), ())),
                                    preferred_element_type=F32)
            num = jnp.dot(sw.astype(BF16), vcb, preferred_element_type=F32) + inter * qcmem
            den = jnp.sum(sw, axis=-1, keepdims=True) + inter * jnp.sum(qc * n_prev, axis=-1, keepdims=True)
            hh = num / jnp.maximum(jnp.abs(den), jnp.exp(-m_t))
            h_ref[rs, hs] = hh
            b_last = bc[CHUNK - 1:CHUNK, :]
            m_new = m_t[CHUNK - 1:CHUNK, :]
            w_end = jnp.exp(b_last - bc + ic - m_new)
            decay = jnp.exp(b_last + m_prev - m_new)
            vw = (vc * w_end).astype(BF16)
            c_ref[h] = decay * c_prev + lax.dot_general(vw, kcb, (((0,), (0,)), ((), ())),
                                                        preferred_element_type=F32)
            n_ref[h:h + 1, :] = decay * n_prev + jnp.sum(w_end * kc, axis=0, keepdims=True)
            m_ref[h:h + 1, :] = jnp.broadcast_to(m_new, (1, LANES))

    z = z_ref[0]
    outs = []
    for h in range(HEADS):
        hs = slice(h * HEAD_DIM, (h + 1) * HEAD_DIM)
        hg = jax.nn.sigmoid(z[:, hs]) * h_ref[:, hs]
        mu = jnp.mean(hg, axis=-1, keepdims=True)
        dv = hg - mu
        var = jnp.mean(dv * dv, axis=-1, keepdims=True)
        outs.append(dv * lax.rsqrt(var + EPS))
    hn = jnp.concatenate(outs, axis=-1)
    hn = hn * gain_ref[...] + skip_ref[...] * xc
    o_ref[0] = jnp.dot(hn.astype(BF16), wp_ref[...], preferred_element_type=F32)


def _mlstm(proj3, conv_w, conv_b, w_q, w_k, w_v, gate_bias, gain, skip, w_proj):
    nb, seq, _ = proj3.shape
    tt = min(MLSTM_TT, seq)
    kern = functools.partial(_mlstm_kernel, tt=tt)
    return pl.pallas_call(
        kern,
        grid=(nb, seq // tt),
        in_specs=[
            pl.BlockSpec((1, tt, MLSTM_WIDTH), lambda b, t: (b, t, COL_XM // MLSTM_WIDTH)),
            pl.BlockSpec((1, tt, MLSTM_WIDTH), lambda b, t: (b, t, COL_Z // MLSTM_WIDTH)),
            pl.BlockSpec((1, tt, LANES), lambda b, t: (b, t, COL_GATE // LANES)),
            _const_spec((CONV_WIDTH, MLSTM_WIDTH)),
            _const_spec((1, MLSTM_WIDTH)),
            _const_spec((HEADS, HEAD_DIM, HEAD_DIM)),
            _const_spec((HEADS, HEAD_DIM, HEAD_DIM)),
            _const_spec((HEADS, HEAD_DIM, HEAD_DIM)),
            _const_spec((1, LANES)),
            _const_spec((1, MLSTM_WIDTH)),
            _const_spec((1, MLSTM_WIDTH)),
            _const_spec((MLSTM_WIDTH, D_MODEL)),
        ],
        out_specs=pl.BlockSpec((1, tt, D_MODEL), lambda b, t: (b, t, 0)),
        out_shape=jax.ShapeDtypeStruct((nb, seq, D_MODEL), F32),
        scratch_shapes=[
            pltpu.VMEM((tt + SUBLANES, MLSTM_WIDTH), F32),
            pltpu.VMEM((HEADS, HEAD_DIM, HEAD_DIM), F32),
            pltpu.VMEM((SUBLANES, HEAD_DIM), F32),
            pltpu.VMEM((SUBLANES, LANES), F32),
            pltpu.VMEM((tt, MLSTM_WIDTH), F32),
        ],
        compiler_params=_cparams(("arbitrary", "arbitrary")),
        name="mlstm",
    )(proj3, proj3, proj3, conv_w, conv_b, w_q, w_k, w_v, gate_bias, gain, skip, w_proj)


def _merge_kernel(ga_ref, gb_ref, ya_ref, yb_ref, x_ref, wo_ref, g2_ref, wq_ref, h1_ref, hn_ref, q_ref):
    merged = jax.nn.sigmoid(ga_ref[...]) * ya_ref[...] + jax.nn.sigmoid(gb_ref[...]) * yb_ref[...]
    h1 = x_ref[...] + jnp.dot(merged.astype(BF16), wo_ref[...], preferred_element_type=F32)
    h1_ref[...] = h1.reshape(h1_ref.shape)
    hn = h1 * lax.rsqrt(jnp.mean(h1 * h1, axis=-1, keepdims=True) + EPS) * g2_ref[...]
    hn_ref[...] = hn.reshape(hn_ref.shape)
    q_ref[...] = jnp.dot(hn.astype(BF16), wq_ref[...], preferred_element_type=F32)


def _merge(proj, y_a, y_b, x2, w_out, g2, w_query):
    t = x2.shape[0]
    tm = min(MERGE_TM, t)
    row = lambda c: pl.BlockSpec((tm, D_MODEL), lambda i: (i, c))
    out_sd = jax.ShapeDtypeStruct((t, D_MODEL), F32)
    dense = pl.BlockSpec((tm, ROW_TILES, LANES), lambda i: (i, 0, 0))
    dense_sd = jax.ShapeDtypeStruct((t, ROW_TILES, LANES), F32)
    return pl.pallas_call(
        _merge_kernel,
        grid=(t // tm,),
        in_specs=[row(COL_GA // D_MODEL), row(COL_GB // D_MODEL), row(0), row(0), row(0),
                  _const_spec((D_MODEL, D_MODEL)), _const_spec((1, D_MODEL)), _const_spec((D_MODEL, D_MODEL))],
        out_specs=[dense, dense, row(0)],
        out_shape=[dense_sd, dense_sd, out_sd],
        compiler_params=_cparams(("arbitrary",)),
        name="merge",
    )(proj, proj, y_a, y_b, x2, w_out, g2, w_query)


_CAND_ROWS = [PEER_TOPK // (i + 1) for i in range(PEER_TOPK)]
_CAND_OFFS = [int(v) for v in np.cumsum([0] + _CAND_ROWS[:-1])]
_NCAND = int(sum(_CAND_ROWS))
_NCAND_PAD = 56


def _extract_topk(s, payload, k, val_ref, pay_ref):
    n = s.shape[0]
    big = jnp.int32(2 ** 30)
    for r in range(k):
        m = jnp.max(s, axis=0, keepdims=True)
        sel = jnp.min(jnp.where(s == m, payload, big), axis=0, keepdims=True)
        val_ref[r:r + 1, :] = m
        pay_ref[r:r + 1, :] = sel
        s = jnp.where(payload == sel, -jnp.inf, s)


def _route_kernel(q_ref, k1_ref, k2_ref, e_ref, g_ref, rec_ref, cnt_ref,
                  v1_ref, i1_ref, v2_ref, i2_ref, cv_ref, ce_ref, cp_ref, tv_ref, tp_ref, ea_ref, gate_ref,
                  rows_ref, lhs_ref, *, tb):
    key_iota = lax.broadcasted_iota(jnp.int32, (PEER_NKEYS, tb), 0)
    cand_iota = lax.broadcasted_iota(jnp.int32, (_NCAND_PAD, tb), 0)
    for h in range(PEER_HEADS):
        q1 = q_ref[:, h * 2 * PEER_HALF:h * 2 * PEER_HALF + PEER_HALF].astype(BF16)
        q2 = q_ref[:, h * 2 * PEER_HALF + PEER_HALF:(h + 1) * 2 * PEER_HALF].astype(BF16)
        s1 = lax.dot_general(k1_ref[h], q1, (((1,), (1,)), ((), ())), preferred_element_type=F32)
        s2 = lax.dot_general(k2_ref[h], q2, (((1,), (1,)), ((), ())), preferred_element_type=F32)
        _extract_topk(s1, key_iota, PEER_TOPK, v1_ref, i1_ref)
        _extract_topk(s2, key_iota, PEER_TOPK, v2_ref, i2_ref)
        cv_ref[...] = jnp.full((_NCAND_PAD, tb), -jnp.inf, F32)
        ce_ref[...] = jnp.zeros((_NCAND_PAD, tb), jnp.int32)
        cp_ref[...] = cand_iota + jnp.int32(1 << 20)
        for i in range(PEER_TOPK):
            n_i, off = _CAND_ROWS[i], _CAND_OFFS[i]
            cv_ref[off:off + n_i, :] = v1_ref[i:i + 1, :] + v2_ref[0:n_i, :]
            ce_ref[off:off + n_i, :] = i1_ref[i:i + 1, :] * PEER_NKEYS + i2_ref[0:n_i, :]
            cp_ref[off:off + n_i, :] = i * PEER_TOPK + lax.broadcasted_iota(jnp.int32, (n_i, tb), 0)
        cv = cv_ref[...]
        cp = cp_ref[...]
        _extract_topk(cv, cp, PEER_TOPK, tv_ref, tp_ref)
        ce = ce_ref[...]
        tv = tv_ref[...]
        ex = jnp.exp(tv - tv[0:1, :])
        gate_ref[h * PEER_TOPK:(h + 1) * PEER_TOPK, :] = ex / jnp.sum(ex, axis=0, keepdims=True)
        for r in range(PEER_TOPK):
            ea_ref[h * PEER_TOPK + r:h * PEER_TOPK + r + 1, :] = jnp.sum(
                jnp.where(cp == tp_ref[r:r + 1, :], ce, 0), axis=0, keepdims=True)

    e_all = ea_ref[...]
    g_all = gate_ref[...]
    chunk = e_all >> PEER_SHIFT
    e_loc = e_all & (PEER_EC - 1)
    ri = lax.broadcasted_iota(jnp.int32, (HK, HK), 0)
    ci = lax.broadcasted_iota(jnp.int32, (HK, HK), 1)
    lower = (ci < ri).astype(BF16)
    dest = jnp.zeros((HK, tb), F32)
    gstart = jnp.zeros((1, tb), F32)
    rows_ref[...] = jnp.zeros(rows_ref.shape, F32)
    for c in range(PEER_NCHUNK):
        mask = chunk == c
        maskf = mask.astype(F32)
        rank = jnp.dot(lower, mask.astype(BF16), preferred_element_type=F32)
        grp = jnp.floor(rank * (1.0 / PEER_G))
        dest = dest + maskf * ((rank - grp * PEER_G) * PEER_TGROUPS + gstart + grp)
        cnt = jnp.sum(maskf, axis=0, keepdims=True).astype(jnp.int32)
        ng = ((cnt + (PEER_G - 1)) >> 3).astype(F32)
        rows_ref[c:c + 1, :] = ng
        rows_ref[SUBLANES + c:SUBLANES + c + 1, :] = gstart
        gstart = gstart + ng
    dest_i = dest.astype(jnp.int32)
    for k in range(PEER_SLOTS):
        sel = dest_i == k
        e_ref[k:k + 1, :] = jnp.sum(jnp.where(sel, e_loc, 0), axis=0, keepdims=True)
        g_ref[k:k + 1, :] = jnp.sum(jnp.where(sel, g_all, 0.0), axis=0, keepdims=True)

    ng8 = rows_ref[0:SUBLANES, :]
    ti = lax.broadcasted_iota(jnp.int32, (tb, tb), 0)
    tj = lax.broadcasted_iota(jnp.int32, (tb, tb), 1)
    before = (ti < tj).astype(BF16)
    off8 = jnp.dot(ng8.astype(BF16), before, preferred_element_type=F32)
    rows_ref[2 * SUBLANES:3 * SUBLANES, :] = off8
    n8 = jnp.sum(ng8, axis=1, keepdims=True).astype(jnp.int32)
    npad8 = (-n8) & (PEER_LIST_ALIGN - 1)
    cnt_ref[...] = jnp.broadcast_to(n8 + npad8, (SUBLANES, LANES))
    cols = rows_ref[...].T
    tok_row = lax.broadcasted_iota(jnp.int32, (1, tb), 1).astype(F32)
    smax = tb * (HK // PEER_G)
    rec_ref[...] = jnp.zeros(rec_ref.shape, jnp.int32)
    for c in range(PEER_NCHUNK):
        off_row = off8[c:c + 1, :]
        off_hi = jnp.floor(off_row * (1.0 / 64.0))
        lhs_ref[...] = jnp.zeros(lhs_ref.shape, F32)
        lhs_ref[0:1, :] = tok_row
        lhs_ref[1:2, :] = rows_ref[SUBLANES + c:SUBLANES + c + 1, :]
        lhs_ref[2:3, :] = off_hi
        lhs_ref[3:4, :] = off_row - 64.0 * off_hi
        lhs = lhs_ref[...].astype(BF16)
        ng_col = cols[:, c:c + 1]
        off_col = cols[:, 2 * SUBLANES + c:2 * SUBLANES + c + 1]
        n_c = n8[c:c + 1, :].astype(F32)
        npad_c = npad8[c:c + 1, :].astype(F32)
        for j in range(smax // ROUTE_ST):
            s = (lax.broadcasted_iota(jnp.int32, (1, ROUTE_ST), 1) + j * ROUTE_ST).astype(F32) - npad_c
            onehot = jnp.logical_and(off_col <= s, s < off_col + ng_col).astype(BF16)
            r = jnp.dot(lhs, onehot, preferred_element_type=F32)
            tok = r[0:1, :]
            real = jnp.logical_and(s >= 0.0, s < n_c)
            grp = jnp.where(real, r[1:2, :] + s - (r[2:3, :] * 64.0 + r[3:4, :]), PEER_TGROUPS - 1.0)
            rec = tok.astype(jnp.int32) * 65536 + (grp * tb + tok).astype(jnp.int32)
            rec_ref[c:c + 1, j * ROUTE_ST:(j + 1) * ROUTE_ST] = rec


def _route(qr, key1, key2):
    t = qr.shape[0]
    tb = min(PEER_TB, t)
    nblk = t // tb
    smax = tb * (HK // PEER_G)
    kern = functools.partial(_route_kernel, tb=tb)
    col = lambda r: pl.BlockSpec((r, tb), lambda i: (0, i))
    f_s = lambda r: pltpu.VMEM((r, tb), F32)
    i_s = lambda r: pltpu.VMEM((r, tb), jnp.int32)
    return pl.pallas_call(
        kern,
        grid=(nblk,),
        in_specs=[pl.BlockSpec((tb, D_MODEL), lambda i: (i, 0)),
                  _const_spec((PEER_HEADS, PEER_NKEYS, PEER_HALF)),
                  _const_spec((PEER_HEADS, PEER_NKEYS, PEER_HALF))],
        out_specs=[col(PEER_SLOTS), col(PEER_SLOTS),
                   pl.BlockSpec((SUBLANES, smax), lambda i: (i, 0)),
                   pl.BlockSpec((SUBLANES, LANES), lambda i: (i, 0))],
        out_shape=[jax.ShapeDtypeStruct((PEER_SLOTS, t), jnp.int32), jax.ShapeDtypeStruct((PEER_SLOTS, t), F32),
                   jax.ShapeDtypeStruct((nblk * SUBLANES, smax), jnp.int32),
                   jax.ShapeDtypeStruct((nblk * SUBLANES, LANES), jnp.int32)],
        scratch_shapes=[f_s(PEER_TOPK), i_s(PEER_TOPK), f_s(PEER_TOPK), i_s(PEER_TOPK),
                        f_s(_NCAND_PAD), i_s(_NCAND_PAD), i_s(_NCAND_PAD),
                        f_s(PEER_TOPK), i_s(PEER_TOPK), i_s(HK), f_s(HK),
                        f_s(LANES), f_s(SUBLANES)],
        compiler_params=_cparams(("arbitrary",)),
        name="route",
    )(qr, key1, key2)


def _peer_kernel(c_ref, cnt_ref, e_hbm, g_hbm, rec_hbm, t_ref, u_ref, v_ref, acc_ref, o_ref,
                 rec_sm, sems, a0_ref, a1_ref, *lists_and_partials, tb):
    c = c_ref[0]
    b = pl.program_id(0)
    ngs = tb * PEER_TGROUPS
    e_sms = lists_and_partials[0:PEER_G]
    g_sms = lists_and_partials[PEER_G:2 * PEER_G]
    p_refs = lists_and_partials[2 * PEER_G:]
    copies = [pltpu.make_async_copy(rec_hbm.at[b * SUBLANES + c], rec_sm, sems.at[0])]
    for k in range(PEER_G):
        copies.append(pltpu.make_async_copy(e_hbm.at[k, pl.ds(b * ngs, ngs)], e_sms[k], sems.at[1 + k]))
        copies.append(pltpu.make_async_copy(g_hbm.at[k, pl.ds(b * ngs, ngs)], g_sms[k], sems.at[1 + PEER_G + k]))
    for cp in copies:
        cp.start()
    o_ref[...] = acc_ref[...]
    for cp in copies:
        cp.wait()
    n = cnt_ref[b * PEER_NCHUNK + c]

    def tree_sum(xs):
        while len(xs) > 1:
            xs = [xs[i] + xs[i + 1] for i in range(0, len(xs) - 1, 2)] + ([xs[-1]] if len(xs) % 2 else [])
        return xs[0]

    def decode(s0, width):
        recs = [rec_sm[s0 + j] for j in range(width)]
        return [r >> 16 for r in recs], [r & 0xFFFF for r in recs]

    def scores(s0, width):
        toks, gss = decode(s0, width)
        for j in range(width):
            t_lo = t_ref[toks[j], 0:SUBLANES, :]
            t_hi = t_ref[toks[j], SUBLANES:, :]
            for k in range(PEER_G):
                u = u_ref[e_sms[k][gss[j]]].astype(F32)
                p_refs[j][k * SUBLANES:(k + 1) * SUBLANES, :] = t_lo * u[0:SUBLANES] + t_hi * u[SUBLANES:]
        rs = [tree_sum([p_refs[j][pl.ds(i, PEER_G, stride=SUBLANES), :] for i in range(SUBLANES)])
              for j in range(width)]
        return jnp.concatenate(rs, axis=0)

    def activate(partials):
        score = jnp.sum(partials, axis=-1, keepdims=True)
        return jnp.broadcast_to(jax.nn.gelu(score), partials.shape)

    def store_act(a_ref, act):
        a_ref[...] = act

    def update(a_ref, s0, width, carry):
        prev_tok, acc_lo, acc_hi = carry
        toks, gss = decode(s0, width)
        for j in range(width):
            lo, hi = [], []
            for k in range(PEER_G):
                pair = j * PEER_G + k
                a = a_ref[pair:pair + 1, :] * g_sms[k][gss[j]]
                v = v_ref[e_sms[k][gss[j]]].astype(F32)
                lo.append(a * v[0:SUBLANES])
                hi.append(a * v[SUBLANES:])
            new = toks[j] != prev_tok
            acc_lo = jnp.where(new, acc_ref[toks[j], 0:SUBLANES, :], acc_lo) + tree_sum(lo)
            acc_hi = jnp.where(new, acc_ref[toks[j], SUBLANES:, :], acc_hi) + tree_sum(hi)
            o_ref[toks[j], 0:SUBLANES, :] = acc_lo
            o_ref[toks[j], SUBLANES:, :] = acc_hi
            prev_tok = toks[j]
        return prev_tok, acc_lo, acc_hi

    w = PEER_UNROLL
    nb = n // w
    last = jnp.maximum(nb - 1, 0)
    zero = jnp.zeros((SUBLANES, LANES), F32)
    store_act(a0_ref, activate(scores(0, w)))
    partials = scores(jnp.minimum(1, last) * w, w)

    def step(i, a_store, a_load, partials, cr):
        act = activate(partials)
        partials = scores(jnp.minimum(i, last) * w, w)
        cr = update(a_load, (i - 2) * w, w, cr)
        store_act(a_store, act)
        return partials, cr

    def two_steps(h, state):
        partials, cr = state[0], state[1:]
        i = 2 * h + 2
        partials, cr = step(i, a1_ref, a0_ref, partials, cr)
        partials, cr = step(i + 1, a0_ref, a1_ref, partials, cr)
        return (partials,) + cr

    lax.fori_loop(0, nb // 2, two_steps, (partials, jnp.int32(-1), zero, zero))


def _peer(e_flat, g_flat, rec, cnt, t3, u3, v3, acc3):
    t = t3.shape[0]
    tb = min(PEER_TB, t)
    kern = functools.partial(_peer_kernel, tb=tb)
    tok_spec = pl.BlockSpec((tb, ROW_TILES, LANES), lambda b, c, n: (b, 0, 0))
    tab_spec = pl.BlockSpec((PEER_EC, ROW_TILES, LANES), lambda b, c, n: (c[0], 0, 0),
                            pipeline_mode=pl.Buffered(1))
    call = pl.pallas_call(
        kern,
        grid_spec=pltpu.PrefetchScalarGridSpec(
            num_scalar_prefetch=2,
            grid=(t // tb,),
            in_specs=[pl.BlockSpec(memory_space=pl.ANY), pl.BlockSpec(memory_space=pl.ANY),
                      pl.BlockSpec(memory_space=pl.ANY), tok_spec, tab_spec, tab_spec, tok_spec],
            out_specs=tok_spec,
            scratch_shapes=[
                pltpu.SMEM((rec.shape[1],), jnp.int32),
                pltpu.SemaphoreType.DMA((1 + 2 * PEER_G,)),
                pltpu.VMEM((PEER_UNROLL * PEER_G, LANES), F32),
                pltpu.VMEM((PEER_UNROLL * PEER_G, LANES), F32),
            ] + [pltpu.SMEM((tb * PEER_TGROUPS,), jnp.int32)] * PEER_G
              + [pltpu.SMEM((tb * PEER_TGROUPS,), F32)] * PEER_G
              + [pltpu.VMEM((PEER_G * SUBLANES, LANES), F32)] * PEER_UNROLL,
        ),
        out_shape=jax.ShapeDtypeStruct(acc3.shape, F32),
        input_output_aliases={8: 0},
        compiler_params=_cparams(("arbitrary",)),
        name="peer",
    )
    acc = acc3
    for c in range(PEER_NCHUNK):
        acc = call(jnp.full((1,), c, jnp.int32), cnt, e_flat, g_flat, rec, t3, u3, v3, acc)
    return acc


def _final_kernel(a_ref, g_ref, o_ref):
    a = a_ref[...].reshape(o_ref.shape)
    ms = jnp.mean(a * a, axis=-1, keepdims=True)
    o_ref[...] = a * lax.rsqrt(ms + EPS) * g_ref[...]


def _final(acc2, g):
    t = acc2.shape[0]
    tm = min(FINAL_TM, t)
    return pl.pallas_call(
        _final_kernel,
        grid=(t // tm,),
        in_specs=[pl.BlockSpec((tm, ROW_TILES, LANES), lambda i: (i, 0, 0)),
                  pl.BlockSpec((1, D_MODEL), lambda i: (0, 0))],
        out_specs=pl.BlockSpec((tm, D_MODEL), lambda i: (i, 0)),
        out_shape=jax.ShapeDtypeStruct((t, D_MODEL), F32),
        compiler_params=_cparams(("arbitrary",)),
        name="final",
    )(acc2, g)


def _layer(x, norm1_g, w_in, lam_re, lam_im, log_dt, b_re, b_im, c_re, c_im, d_skip, w_glu, conv_w, conv_b,
           w_q, w_k, w_v, b_i, b_f, mh_gain, mlstm_skip, w_mlstm_out, w_out, norm2_g, w_query, key1, key2,
           expert_u, expert_v):
    nb, seq, d = x.shape
    t = nb * seq
    x2 = x.reshape(t, d)
    row = lambda a: a.reshape(1, -1).astype(F32)

    o_ssm, o_xm, o_z, o_i, o_f, o_ga, o_gb = 0, 512, 2048, 3584, 3588, 3592, 5640
    zeros = lambda n: jnp.zeros((d, n), w_in.dtype)
    w_cat = jnp.concatenate([
        w_in[:, o_xm:o_z], w_in[:, o_z:o_i], w_in[:, o_ssm:o_xm], w_in[:, o_i:o_ga],
        zeros(COL_GA - COL_GATE - 2 * HEADS), w_in[:, o_ga:o_gb], w_in[:, o_gb:]], axis=1).astype(BF16)
    proj = _inproj(x2, row(norm1_g), w_cat)
    proj3 = proj.reshape(nb, seq, PROJ_W)

    bblk, a_re, a_im, cblk = _s5_params(lam_re, lam_im, log_dt, b_re, b_im, c_re, c_im)
    y_a = _s5(proj3, bblk, a_re, a_im, cblk, row(d_skip), w_glu.astype(BF16))

    gate_bias = jnp.concatenate([b_i.astype(F32), b_f.astype(F32), jnp.zeros((LANES - 2 * HEADS,), F32)]).reshape(1, LANES)
    y_b = _mlstm(proj3, conv_w.astype(F32), row(conv_b), w_q.astype(BF16), w_k.astype(BF16), w_v.astype(BF16),
                 gate_bias, row(mh_gain), row(mlstm_skip), w_mlstm_out.astype(BF16))

    h1, hn2, qr = _merge(proj, y_a.reshape(t, d), y_b.reshape(t, d), x2, w_out.astype(BF16), row(norm2_g),
                         w_query.astype(BF16))

    e_t, g_t, rec, cnt8 = _route(qr, key1.astype(BF16), key2.astype(BF16))
    tb = min(PEER_TB, t)
    by_pair = lambda a: a.reshape(PEER_G, PEER_TGROUPS, t // tb, tb).transpose(0, 2, 1, 3).reshape(PEER_G, -1)
    e_flat = by_pair(e_t)
    g_flat = by_pair(g_t)
    cnt = cnt8[:, 0].reshape(-1, SUBLANES)[:, :PEER_NCHUNK].reshape(-1)

    dense = lambda a: a.reshape(a.shape[0], ROW_TILES, LANES)
    return _peer(e_flat, g_flat, rec, cnt, hn2, dense(expert_u.astype(BF16)), dense(expert_v.astype(BF16)), h1)


def kernel(x, norm1_g, w_in, lam_re, lam_im, log_dt, b_re, b_im, c_re, c_im, d_skip, w_glu, conv_w, conv_b, w_q, w_k, w_v, b_i, b_f, mh_gain, mlstm_skip, w_mlstm_out, w_out, norm2_g, w_query, key1, key2, expert_u, expert_v, norm_f_g):
    depth = w_in.shape[0]
    nb, seq, d = x.shape
    h = x
    for l in range(depth):
        acc = _layer(h, norm1_g[l], w_in[l], lam_re[l], lam_im[l], log_dt[l], b_re[l], b_im[l], c_re[l], c_im[l],
                     d_skip[l], w_glu[l], conv_w[l], conv_b[l], w_q[l], w_k[l], w_v[l], b_i[l], b_f[l], mh_gain[l],
                     mlstm_skip[l], w_mlstm_out[l], w_out[l], norm2_g[l], w_query[l], key1[l], key2[l],
                     expert_u[l], expert_v[l])
        h = acc.reshape(nb, seq, d) if l + 1 < depth else None
    out = _final(acc, norm_f_g.reshape(1, d).astype(F32))
    return out.reshape(nb, seq, d)
```

```python
import functools
import math

import jax
import jax.numpy as jnp
import numpy as np
from jax import lax
from jax.experimental import pallas as pl
from jax.experimental.pallas import tpu as pltpu

F32 = jnp.float32
BF16 = jnp.bfloat16
EPS = 1e-6

D_MODEL = 2048
SSM_WIDTH = 512
SSM_GROUP = 16
SSM_GROUPS = 32
SSM_STATE = 64
N_STATE = SSM_GROUPS * SSM_STATE
MLSTM_WIDTH = 1536
HEADS = 4
HEAD_DIM = 384
CONV_WIDTH = 4
CHUNK = 64
PEER_HEADS = 8
PEER_HALF = 128
PEER_NKEYS = 128
PEER_TOPK = 16
PEER_EXPERTS = PEER_NKEYS * PEER_NKEYS
HK = PEER_HEADS * PEER_TOPK

LANES = 128
SUBLANES = 8
ROW_TILES = D_MODEL // LANES

COL_XM = 0
COL_Z = 1536
COL_SSM = 3072
COL_GATE = 3584
COL_GA = 4096
COL_GB = 6144
PROJ_W = 8192

INPROJ_TM = 1024
INPROJ_TN = 1024
S5_LT = 32
MLSTM_TT = 256
MERGE_TM = 256
ROUTE_ST = 512
PEER_EC = 4096
PEER_NCHUNK = PEER_EXPERTS // PEER_EC
PEER_SHIFT = int(math.log2(PEER_EC))
PEER_TB = 128
PEER_G = 8
PEER_TGROUPS = 20
PEER_SLOTS = PEER_TGROUPS * PEER_G
PEER_UNROLL = 4
PEER_LIST_ALIGN = 2 * PEER_UNROLL
FINAL_TM = 512
assert PEER_G == SUBLANES and HK + PEER_NCHUNK * (PEER_G - 1) <= PEER_SLOTS

VMEM_LIMIT = 56 * 1024 * 1024


def _cparams(sem):
    return pltpu.CompilerParams(dimension_semantics=sem, vmem_limit_bytes=VMEM_LIMIT)


def _const_spec(shape):
    nd = len(shape)
    return pl.BlockSpec(shape, lambda *_: (0,) * nd, pipeline_mode=pl.Buffered(1))


def _inproj_kernel(x_ref, g_ref, w_ref, o_ref, hn_ref):
    @pl.when(pl.program_id(1) == 0)
    def _():
        x = x_ref[...]
        y = x * lax.rsqrt(jnp.mean(x * x, axis=-1, keepdims=True) + EPS) * g_ref[...]
        hn_ref[...] = y.astype(BF16)

    o_ref[...] = jnp.dot(hn_ref[...], w_ref[...], preferred_element_type=F32)


def _inproj(x2, g, w_cat):
    t = x2.shape[0]
    tm = min(INPROJ_TM, t)
    return pl.pallas_call(
        _inproj_kernel,
        grid=(t // tm, PROJ_W // INPROJ_TN),
        in_specs=[
            pl.BlockSpec((tm, D_MODEL), lambda i, n: (i, 0)),
            pl.BlockSpec((1, D_MODEL), lambda i, n: (0, 0)),
            pl.BlockSpec((D_MODEL, INPROJ_TN), lambda i, n: (0, n)),
        ],
        out_specs=pl.BlockSpec((tm, INPROJ_TN), lambda i, n: (i, n)),
        out_shape=jax.ShapeDtypeStruct((t, PROJ_W), F32),
        scratch_shapes=[pltpu.VMEM((tm, D_MODEL), BF16)],
        compiler_params=_cparams(("arbitrary", "arbitrary")),
        name="inproj",
    )(x2, g, w_cat)


def _s5_kernel(u_ref, perm_ref, permt_ref, bblk_ref, are_ref, aim_ref, cblk_ref, d_ref, wglu_ref, o_ref,
               x_ref, st_ref, *, nb, lt):
    @pl.when(pl.program_id(0) == 0)
    def _():
        st_ref[...] = jnp.zeros_like(st_ref)

    rows = nb * lt
    u = u_ref[...].reshape(rows, SSM_WIDTH)
    ut = jnp.dot(perm_ref[...], u.astype(BF16), preferred_element_type=F32).astype(BF16)
    bu = jnp.dot(ut, bblk_ref[...], preferred_element_type=F32)
    n_tiles = N_STATE // LANES
    for j in range(2 * n_tiles):
        x_ref[j] = bu[:, j * LANES:(j + 1) * LANES]

    tiles_per_pass = 4
    for t0 in range(0, n_tiles, tiles_per_pass):
        tiles = range(t0, t0 + tiles_per_pass)
        a_re = [are_ref[:, j * LANES:(j + 1) * LANES] for j in tiles]
        a_im = [aim_ref[:, j * LANES:(j + 1) * LANES] for j in tiles]

        def step(s, carry):
            r = pl.ds(pl.multiple_of(s * nb, nb), nb)
            out = []
            for i, j in enumerate(tiles):
                s_re, s_im = carry[2 * i], carry[2 * i + 1]
                n_re = a_re[i] * s_re - a_im[i] * s_im + x_ref[j, r, :]
                n_im = a_re[i] * s_im + a_im[i] * s_re + x_ref[n_tiles + j, r, :]
                x_ref[j, r, :] = n_re
                x_ref[n_tiles + j, r, :] = n_im
                out += [n_re, n_im]
            return tuple(out)

        init = []
        for j in tiles:
            init += [st_ref[j], st_ref[n_tiles + j]]
        fin = lax.fori_loop(0, lt, step, tuple(init))
        for i, j in enumerate(tiles):
            st_ref[j] = fin[2 * i]
            st_ref[n_tiles + j] = fin[2 * i + 1]

    xs = jnp.concatenate([x_ref[j].astype(BF16) for j in range(2 * n_tiles)], axis=-1)
    z = jnp.dot(xs, cblk_ref[...], preferred_element_type=F32)
    z = jnp.dot(permt_ref[...], z, preferred_element_type=F32, precision=lax.Precision.HIGHEST)
    y = z + d_ref[...] * u
    y = jax.nn.gelu(y)
    vg = jnp.dot(y.astype(BF16), wglu_ref[...], preferred_element_type=F32)
    out = vg[:, :D_MODEL] * jax.nn.sigmoid(vg[:, D_MODEL:])
    o_ref[...] = out.reshape(nb, lt, D_MODEL)


def _s5(proj3, bblk, a_re, a_im, cblk, d_skip, w_glu):
    nb, seq, _ = proj3.shape
    lt = min(S5_LT, seq)
    kern = functools.partial(_s5_kernel, nb=nb, lt=lt)
    src = (jnp.arange(nb)[None, :] * lt + jnp.arange(lt)[:, None]).reshape(-1)
    perm = jax.nn.one_hot(src, nb * lt, dtype=F32)
    return pl.pallas_call(
        kern,
        grid=(seq // lt,),
        in_specs=[
            pl.BlockSpec((nb, lt, SSM_WIDTH), lambda c: (0, c, COL_SSM // SSM_WIDTH)),
            _const_spec((nb * lt, nb * lt)),
            _const_spec((nb * lt, nb * lt)),
            _const_spec((SSM_WIDTH, 2 * N_STATE)),
            _const_spec((1, N_STATE)),
            _const_spec((1, N_STATE)),
            _const_spec((2 * N_STATE, SSM_WIDTH)),
            _const_spec((1, SSM_WIDTH)),
            _const_spec((SSM_WIDTH, 2 * D_MODEL)),
        ],
        out_specs=pl.BlockSpec((nb, lt, D_MODEL), lambda c: (0, c, 0)),
        out_shape=jax.ShapeDtypeStruct((nb, seq, D_MODEL), F32),
        scratch_shapes=[pltpu.VMEM((2 * N_STATE // LANES, nb * lt, LANES), F32),
                        pltpu.VMEM((2 * N_STATE // LANES, nb, LANES), F32)],
        compiler_params=_cparams(("arbitrary",)),
        name="s5",
    )(proj3, perm.astype(BF16), perm.T, bblk, a_re, a_im, cblk, d_skip, w_glu)


def _s5_params(lam_re, lam_im, log_dt, b_re, b_im, c_re, c_im):
    g, p, h = SSM_GROUPS, SSM_STATE, SSM_GROUP
    lam = lax.complex(lam_re.astype(F32), lam_im.astype(F32))
    dt = jnp.exp(log_dt.astype(F32))[:, None]
    a_bar = jnp.exp(lam * dt)
    b_bar = ((a_bar - 1.0) / lam)[..., None] * lax.complex(b_re.astype(F32), b_im.astype(F32))
    eye = jnp.eye(g, dtype=F32)
    bb_re = jnp.einsum("gph,gk->ghkp", jnp.real(b_bar), eye).reshape(g * h, g * p)
    bb_im = jnp.einsum("gph,gk->ghkp", jnp.imag(b_bar), eye).reshape(g * h, g * p)
    bblk = jnp.concatenate([bb_re, bb_im], axis=1).astype(BF16)
    cc_re = jnp.einsum("ghp,gk->kpgh", c_re.astype(F32), eye).reshape(g * p, g * h)
    cc_im = jnp.einsum("ghp,gk->kpgh", c_im.astype(F32), eye).reshape(g * p, g * h)
    cblk = jnp.concatenate([cc_re, -cc_im], axis=0).astype(BF16)
    return bblk, jnp.real(a_bar).reshape(1, g * p), jnp.imag(a_bar).reshape(1, g * p), cblk


def _mlstm_kernel(xm_ref, z_ref, gt_ref, cw_ref, cb_ref, wq_ref, wk_ref, wv_ref, gb_ref, gain_ref, skip_ref,
                  wp_ref, o_ref, cbuf_ref, c_ref, n_ref, m_ref, h_ref, *, tt):
    halo = SUBLANES

    @pl.when(pl.program_id(1) == 0)
    def _():
        cbuf_ref[0:halo, :] = jnp.zeros((halo, MLSTM_WIDTH), F32)
        c_ref[...] = jnp.zeros_like(c_ref)
        n_ref[...] = jnp.zeros_like(n_ref)
        m_ref[...] = jnp.zeros_like(m_ref)

    @pl.when(pl.program_id(1) > 0)
    def _():
        cbuf_ref[0:halo, :] = cbuf_ref[tt:tt + halo, :]

    xm = xm_ref[0]
    cbuf_ref[halo:halo + tt, :] = xm
    conv = cb_ref[...]
    for j in range(CONV_WIDTH):
        off = halo - (CONV_WIDTH - 1) + j
        conv = conv + cbuf_ref[off:off + tt, :] * cw_ref[j:j + 1, :]
    xc = conv * jax.nn.sigmoid(conv)
    xcb = xc.astype(BF16)
    xmb = xm.astype(BF16)

    row_i = lax.broadcasted_iota(jnp.int32, (CHUNK, CHUNK), 0)
    col_i = lax.broadcasted_iota(jnp.int32, (CHUNK, CHUNK), 1)
    causal = col_i <= row_i
    tril = causal.astype(F32)
    triu = (row_i <= col_i).astype(F32)

    qs, ks, vs = [], [], []
    for h in range(HEADS):
        hs = slice(h * HEAD_DIM, (h + 1) * HEAD_DIM)
        qs.append(jnp.dot(xcb[:, hs], wq_ref[h], preferred_element_type=F32))
        ks.append(jnp.dot(xcb[:, hs], wk_ref[h], preferred_element_type=F32) * (HEAD_DIM ** -0.5))
        vs.append(jnp.dot(xmb[:, hs], wv_ref[h], preferred_element_type=F32))

    for j in range(tt // CHUNK):
        rs = slice(j * CHUNK, (j + 1) * CHUNK)
        gcol = gt_ref[0, rs, :] + gb_ref[...]
        lf_col = jax.nn.log_sigmoid(gcol)
        bcum_col = jnp.dot(tril, lf_col, preferred_element_type=F32, precision=lax.Precision.HIGHEST)
        grow = gcol.T
        lf_row = jax.nn.log_sigmoid(grow[0:SUBLANES, :])
        bcum_row = jnp.dot(lf_row, triu, preferred_element_type=F32, precision=lax.Precision.HIGHEST)
        for h in range(HEADS):
            hs = slice(h * HEAD_DIM, (h + 1) * HEAD_DIM)
            qc, kc, vc = qs[h][rs], ks[h][rs], vs[h][rs]
            bc = bcum_col[:, HEADS + h:HEADS + h + 1]
            br = bcum_row[HEADS + h:HEADS + h + 1, :]
            ic = gcol[:, h:h + 1]
            ir = grow[h:h + 1, :]
            m_prev = m_ref[h:h + 1, 0:1]
            log_w = jnp.where(causal, bc - br + ir, -jnp.inf)
            log_inter = bc + m_prev
            m_t = jnp.maximum(log_inter, jnp.max(log_w, axis=-1, keepdims=True))
            w = jnp.exp(log_w - m_t)
            inter = jnp.exp(log_inter - m_t)
            qcb, kcb, vcb = qc.astype(BF16), kc.astype(BF16), vc.astype(BF16)
            s = lax.dot_general(qcb, kcb, (((1,), (1,)), ((), ())), preferred_element_type=F32)
            sw = s * w
            c_prev = c_ref[h]
            n_prev = n_ref[h:h + 1, :]
            qcmem = lax.dot_general(qcb, c_prev.astype(BF16), (((1,), (1,)), ((), ())),
                                    preferred_element_type=F32)
            num = jnp.dot(sw.astype(BF16), vcb, preferred_element_type=F32) + inter * qcmem
            den = jnp.sum(sw, axis=-1, keepdims=True) + inter * jnp.sum(qc * n_prev, axis=-1, keepdims=True)
            hh = num / jnp.maximum(jnp.abs(den), jnp.exp(-m_t))
            h_ref[rs, hs] = hh
            b_last = bc[CHUNK - 1:CHUNK, :]
            m_new = m_t[CHUNK - 1:CHUNK, :]
            w_end = jnp.exp(b_last - bc + ic - m_new)
            decay = jnp.exp(b_last + m_prev - m_new)
            vw = (vc * w_end).astype(BF16)
            c_ref[h] = decay * c_prev + lax.dot_general(vw, kcb, (((0,), (0,)), ((), ())),
                                                        preferred_element_type=F32)
            n_ref[h:h + 1, :] = decay * n_prev + jnp.sum(w_end * kc, axis=0, keepdims=True)
            m_ref[h:h + 1, :] = jnp.broadcast_to(m_new, (1, LANES))

    z = z_ref[0]
    outs = []
    for h in range(HEADS):
        hs = slice(h * HEAD_DIM, (h + 1) * HEAD_DIM)
        hg = jax.nn.sigmoid(z[:, hs]) * h_ref[:, hs]
        mu = jnp.mean(hg, axis=-1, keepdims=True)
        dv = hg - mu
        var = jnp.mean(dv * dv, axis=-1, keepdims=True)
        outs.append(dv * lax.rsqrt(var + EPS))
    hn = jnp.concatenate(outs, axis=-1)
    hn = hn * gain_ref[...] + skip_ref[...] * xc
    o_ref[0] = jnp.dot(hn.astype(BF16), wp_ref[...], preferred_element_type=F32)


def _mlstm(proj3, conv_w, conv_b, w_q, w_k, w_v, gate_bias, gain, skip, w_proj):
    nb, seq, _ = proj3.shape
    tt = min(MLSTM_TT, seq)
    kern = functools.partial(_mlstm_kernel, tt=tt)
    return pl.pallas_call(
        kern,
        grid=(nb, seq // tt),
        in_specs=[
            pl.BlockSpec((1, tt, MLSTM_WIDTH), lambda b, t: (b, t, COL_XM // MLSTM_WIDTH)),
            pl.BlockSpec((1, tt, MLSTM_WIDTH), lambda b, t: (b, t, COL_Z // MLSTM_WIDTH)),
            pl.BlockSpec((1, tt, LANES), lambda b, t: (b, t, COL_GATE // LANES)),
            _const_spec((CONV_WIDTH, MLSTM_WIDTH)),
            _const_spec((1, MLSTM_WIDTH)),
            _const_spec((HEADS, HEAD_DIM, HEAD_DIM)),
            _const_spec((HEADS, HEAD_DIM, HEAD_DIM)),
            _const_spec((HEADS, HEAD_DIM, HEAD_DIM)),
            _const_spec((1, LANES)),
            _const_spec((1, MLSTM_WIDTH)),
            _const_spec((1, MLSTM_WIDTH)),
            _const_spec((MLSTM_WIDTH, D_MODEL)),
        ],
        out_specs=pl.BlockSpec((1, tt, D_MODEL), lambda b, t: (b, t, 0)),
        out_shape=jax.ShapeDtypeStruct((nb, seq, D_MODEL), F32),
        scratch_shapes=[
            pltpu.VMEM((tt + SUBLANES, MLSTM_WIDTH), F32),
            pltpu.VMEM((HEADS, HEAD_DIM, HEAD_DIM), F32),
            pltpu.VMEM((SUBLANES, HEAD_DIM), F32),
            pltpu.VMEM((SUBLANES, LANES), F32),
            pltpu.VMEM((tt, MLSTM_WIDTH), F32),
        ],
        compiler_params=_cparams(("arbitrary", "arbitrary")),
        name="mlstm",
    )(proj3, proj3, proj3, conv_w, conv_b, w_q, w_k, w_v, gate_bias, gain, skip, w_proj)


def _merge_kernel(ga_ref, gb_ref, ya_ref, yb_ref, x_ref, wo_ref, g2_ref, wq_ref, h1_ref, hn_ref, q_ref):
    merged = jax.nn.sigmoid(ga_ref[...]) * ya_ref[...] + jax.nn.sigmoid(gb_ref[...]) * yb_ref[...]
    h1 = x_ref[...] + jnp.dot(merged.astype(BF16), wo_ref[...], preferred_element_type=F32)
    h1_ref[...] = h1.reshape(h1_ref.shape)
    hn = h1 * lax.rsqrt(jnp.mean(h1 * h1, axis=-1, keepdims=True) + EPS) * g2_ref[...]
    hn_ref[...] = hn.reshape(hn_ref.shape)
    q_ref[...] = jnp.dot(hn.astype(BF16), wq_ref[...], preferred_element_type=F32)


def _merge(proj, y_a, y_b, x2, w_out, g2, w_query):
    t = x2.shape[0]
    tm = min(MERGE_TM, t)
    row = lambda c: pl.BlockSpec((tm, D_MODEL), lambda i: (i, c))
    out_sd = jax.ShapeDtypeStruct((t, D_MODEL), F32)
    dense = pl.BlockSpec((tm, ROW_TILES, LANES), lambda i: (i, 0, 0))
    dense_sd = jax.ShapeDtypeStruct((t, ROW_TILES, LANES), F32)
    return pl.pallas_call(
        _merge_kernel,
        grid=(t // tm,),
        in_specs=[row(COL_GA // D_MODEL), row(COL_GB // D_MODEL), row(0), row(0), row(0),
                  _const_spec((D_MODEL, D_MODEL)), _const_spec((1, D_MODEL)), _const_spec((D_MODEL, D_MODEL))],
        out_specs=[dense, dense, row(0)],
        out_shape=[dense_sd, dense_sd, out_sd],
        compiler_params=_cparams(("arbitrary",)),
        name="merge",
    )(proj, proj, y_a, y_b, x2, w_out, g2, w_query)


_CAND_ROWS = [PEER_TOPK // (i + 1) for i in range(PEER_TOPK)]
_CAND_OFFS = [int(v) for v in np.cumsum([0] + _CAND_ROWS[:-1])]
_NCAND = int(sum(_CAND_ROWS))
_NCAND_PAD = 56


def _extract_topk(s, payload, k, val_ref, pay_ref):
    n = s.shape[0]
    big = jnp.int32(2 ** 30)
    for r in range(k):
        m = jnp.max(s, axis=0, keepdims=True)
        sel = jnp.min(jnp.where(s == m, payload, big), axis=0, keepdims=True)
        val_ref[r:r + 1, :] = m
        pay_ref[r:r + 1, :] = sel
        s = jnp.where(payload == sel, -jnp.inf, s)


def _route_kernel(q_ref, k1_ref, k2_ref, e_ref, g_ref, rec_ref, cnt_ref,
                  v1_ref, i1_ref, v2_ref, i2_ref, cv_ref, ce_ref, cp_ref, tv_ref, tp_ref, ea_ref, gate_ref,
                  rows_ref, lhs_ref, *, tb):
    key_iota = lax.broadcasted_iota(jnp.int32, (PEER_NKEYS, tb), 0)
    cand_iota = lax.broadcasted_iota(jnp.int32, (_NCAND_PAD, tb), 0)
    for h in range(PEER_HEADS):
        q1 = q_ref[:, h * 2 * PEER_HALF:h * 2 * PEER_HALF + PEER_HALF].astype(BF16)
        q2 = q_ref[:, h * 2 * PEER_HALF + PEER_HALF:(h + 1) * 2 * PEER_HALF].astype(BF16)
        s1 = lax.dot_general(k1_ref[h], q1, (((1,), (1,)), ((), ())), preferred_element_type=F32)
        s2 = lax.dot_general(k2_ref[h], q2, (((1,), (1,)), ((), ())), preferred_element_type=F32)
        _extract_topk(s1, key_iota, PEER_TOPK, v1_ref, i1_ref)
        _extract_topk(s2, key_iota, PEER_TOPK, v2_ref, i2_ref)
        cv_ref[...] = jnp.full((_NCAND_PAD, tb), -jnp.inf, F32)
        ce_ref[...] = jnp.zeros((_NCAND_PAD, tb), jnp.int32)
        cp_ref[...] = cand_iota + jnp.int32(1 << 20)
        for i in range(PEER_TOPK):
            n_i, off = _CAND_ROWS[i], _CAND_OFFS[i]
            cv_ref[off:off + n_i, :] = v1_ref[i:i + 1, :] + v2_ref[0:n_i, :]
            ce_ref[off:off + n_i, :] = i1_ref[i:i + 1, :] * PEER_NKEYS + i2_ref[0:n_i, :]
            cp_ref[off:off + n_i, :] = i * PEER_TOPK + lax.broadcasted_iota(jnp.int32, (n_i, tb), 0)
        cv = cv_ref[...]
        cp = cp_ref[...]
        _extract_topk(cv, cp, PEER_TOPK, tv_ref, tp_ref)
        ce = ce_ref[...]
        tv = tv_ref[...]
        ex = jnp.exp(tv - tv[0:1, :])
        gate_ref[h * PEER_TOPK:(h + 1) * PEER_TOPK, :] = ex / jnp.sum(ex, axis=0, keepdims=True)
        for r in range(PEER_TOPK):
            ea_ref[h * PEER_TOPK + r:h * PEER_TOPK + r + 1, :] = jnp.sum(
                jnp.where(cp == tp_ref[r:r + 1, :], ce, 0), axis=0, keepdims=True)

    e_all = ea_ref[...]
    g_all = gate_ref[...]
    chunk = e_all >> PEER_SHIFT
    e_loc = e_all & (PEER_EC - 1)
    ri = lax.broadcasted_iota(jnp.int32, (HK, HK), 0)
    ci = lax.broadcasted_iota(jnp.int32, (HK, HK), 1)
    lower = (ci < ri).astype(BF16)
    dest = jnp.zeros((HK, tb), F32)
    gstart = jnp.zeros((1, tb), F32)
    rows_ref[...] = jnp.zeros(rows_ref.shape, F32)
    for c in range(PEER_NCHUNK):
        mask = chunk == c
        maskf = mask.astype(F32)
        rank = jnp.dot(lower, mask.astype(BF16), preferred_element_type=F32)
        grp = jnp.floor(rank * (1.0 / PEER_G))
        dest = dest + maskf * ((rank - grp * PEER_G) * PEER_TGROUPS + gstart + grp)
        cnt = jnp.sum(maskf, axis=0, keepdims=True).astype(jnp.int32)
        ng = ((cnt + (PEER_G - 1)) >> 3).astype(F32)
        rows_ref[c:c + 1, :] = ng
        rows_ref[SUBLANES + c:SUBLANES + c + 1, :] = gstart
        gstart = gstart + ng
    dest_i = dest.astype(jnp.int32)
    for k in range(PEER_SLOTS):
        sel = dest_i == k
        e_ref[k:k + 1, :] = jnp.sum(jnp.where(sel, e_loc, 0), axis=0, keepdims=True)
        g_ref[k:k + 1, :] = jnp.sum(jnp.where(sel, g_all, 0.0), axis=0, keepdims=True)

    ng8 = rows_ref[0:SUBLANES, :]
    ti = lax.broadcasted_iota(jnp.int32, (tb, tb), 0)
    tj = lax.broadcasted_iota(jnp.int32, (tb, tb), 1)
    before = (ti < tj).astype(BF16)
    off8 = jnp.dot(ng8.astype(BF16), before, preferred_element_type=F32)
    rows_ref[2 * SUBLANES:3 * SUBLANES, :] = off8
    n8 = jnp.sum(ng8, axis=1, keepdims=True).astype(jnp.int32)
    npad8 = (-n8) & (PEER_LIST_ALIGN - 1)
    cnt_ref[...] = jnp.broadcast_to(n8 + npad8, (SUBLANES, LANES))
    cols = rows_ref[...].T
    tok_row = lax.broadcasted_iota(jnp.int32, (1, tb), 1).astype(F32)
    smax = tb * (HK // PEER_G)
    rec_ref[...] = jnp.zeros(rec_ref.shape, jnp.int32)
    for c in range(PEER_NCHUNK):
        off_row = off8[c:c + 1, :]
        off_hi = jnp.floor(off_row * (1.0 / 64.0))
        lhs_ref[...] = jnp.zeros(lhs_ref.shape, F32)
        lhs_ref[0:1, :] = tok_row
        lhs_ref[1:2, :] = rows_ref[SUBLANES + c:SUBLANES + c + 1, :]
        lhs_ref[2:3, :] = off_hi
        lhs_ref[3:4, :] = off_row - 64.0 * off_hi
        lhs = lhs_ref[...].astype(BF16)
        ng_col = cols[:, c:c + 1]
        off_col = cols[:, 2 * SUBLANES + c:2 * SUBLANES + c + 1]
        n_c = n8[c:c + 1, :].astype(F32)
        npad_c = npad8[c:c + 1, :].astype(F32)
        for j in range(smax // ROUTE_ST):
            s = (lax.broadcasted_iota(jnp.int32, (1, ROUTE_ST), 1) + j * ROUTE_ST).astype(F32) - npad_c
            onehot = jnp.logical_and(off_col <= s, s < off_col + ng_col).astype(BF16)
            r = jnp.dot(lhs, onehot, preferred_element_type=F32)
            tok = r[0:1, :]
            real = jnp.logical_and(s >= 0.0, s < n_c)
            grp = jnp.where(real, r[1:2, :] + s - (r[2:3, :] * 64.0 + r[3:4, :]), PEER_TGROUPS - 1.0)
            gslot = (grp * tb + tok).astype(jnp.int32) + (pl.program_id(0) & 1) * (tb * PEER_TGROUPS)
            rec = tok.astype(jnp.int32) * 65536 + gslot
            rec_ref[c:c + 1, j * ROUTE_ST:(j + 1) * ROUTE_ST] = rec


def _route(qr, key1, key2):
    t = qr.shape[0]
    tb = min(PEER_TB, t)
    nblk = t // tb
    smax = tb * (HK // PEER_G)
    kern = functools.partial(_route_kernel, tb=tb)
    col = lambda r: pl.BlockSpec((r, tb), lambda i: (0, i))
    f_s = lambda r: pltpu.VMEM((r, tb), F32)
    i_s = lambda r: pltpu.VMEM((r, tb), jnp.int32)
    return pl.pallas_call(
        kern,
        grid=(nblk,),
        in_specs=[pl.BlockSpec((tb, D_MODEL), lambda i: (i, 0)),
                  _const_spec((PEER_HEADS, PEER_NKEYS, PEER_HALF)),
                  _const_spec((PEER_HEADS, PEER_NKEYS, PEER_HALF))],
        out_specs=[col(PEER_SLOTS), col(PEER_SLOTS),
                   pl.BlockSpec((SUBLANES, smax), lambda i: (i, 0)),
                   pl.BlockSpec((SUBLANES, LANES), lambda i: (i, 0))],
        out_shape=[jax.ShapeDtypeStruct((PEER_SLOTS, t), jnp.int32), jax.ShapeDtypeStruct((PEER_SLOTS, t), F32),
                   jax.ShapeDtypeStruct((nblk * SUBLANES, smax), jnp.int32),
                   jax.ShapeDtypeStruct((nblk * SUBLANES, LANES), jnp.int32)],
        scratch_shapes=[f_s(PEER_TOPK), i_s(PEER_TOPK), f_s(PEER_TOPK), i_s(PEER_TOPK),
                        f_s(_NCAND_PAD), i_s(_NCAND_PAD), i_s(_NCAND_PAD),
                        f_s(PEER_TOPK), i_s(PEER_TOPK), i_s(HK), f_s(HK),
                        f_s(LANES), f_s(SUBLANES)],
        compiler_params=_cparams(("arbitrary",)),
        name="route",
    )(qr, key1, key2)


def _peer_kernel(c_ref, cnt_ref, e_hbm, g_hbm, rec_hbm, t_ref, u_ref, v_ref, acc_ref, o_ref,
                 rec_sm, sems, a0_ref, a1_ref, *lists_and_partials, tb):
    c = c_ref[0]
    b = pl.program_id(0)
    ngs = tb * PEER_TGROUPS
    e_sms = lists_and_partials[0:PEER_G]
    g_sms = lists_and_partials[PEER_G:2 * PEER_G]
    p_refs = lists_and_partials[2 * PEER_G:]
    smax = rec_hbm.shape[1]
    par = b & 1

    def list_copies(blk, half):
        cps = [pltpu.make_async_copy(rec_hbm.at[blk * SUBLANES + c], rec_sm.at[pl.ds(half * smax, smax)],
                                     sems.at[half, 0])]
        for k in range(PEER_G):
            dst = pl.ds(half * ngs, ngs)
            cps.append(pltpu.make_async_copy(e_hbm.at[k, pl.ds(blk * ngs, ngs)], e_sms[k].at[dst],
                                             sems.at[half, 1 + k]))
            cps.append(pltpu.make_async_copy(g_hbm.at[k, pl.ds(blk * ngs, ngs)], g_sms[k].at[dst],
                                             sems.at[half, 1 + PEER_G + k]))
        return cps

    @pl.when(b == 0)
    def _():
        for cp in list_copies(b, par):
            cp.start()

    @pl.when(b + 1 < pl.num_programs(0))
    def _():
        for cp in list_copies(b + 1, 1 - par):
            cp.start()

    o_ref[...] = acc_ref[...]
    for cp in list_copies(b, par):
        cp.wait()
    n = cnt_ref[b * PEER_NCHUNK + c]
    rec0 = par * smax

    def tree_sum(xs):
        while len(xs) > 1:
            xs = [xs[i] + xs[i + 1] for i in range(0, len(xs) - 1, 2)] + ([xs[-1]] if len(xs) % 2 else [])
        return xs[0]

    def decode(s0, width):
        recs = [rec_sm[rec0 + s0 + j] for j in range(width)]
        return [r >> 16 for r in recs], [r & 0xFFFF for r in recs]

    def scores(s0, width):
        toks, gss = decode(s0, width)
        for j in range(width):
            t_lo = t_ref[toks[j], 0:SUBLANES, :]
            t_hi = t_ref[toks[j], SUBLANES:, :]
            for k in range(PEER_G):
                u = u_ref[e_sms[k][gss[j]]].astype(F32)
                p_refs[j][k * SUBLANES:(k + 1) * SUBLANES, :] = t_lo * u[0:SUBLANES] + t_hi * u[SUBLANES:]
        rs = [tree_sum([p_refs[j][pl.ds(i, PEER_G, stride=SUBLANES), :] for i in range(SUBLANES)])
              for j in range(width)]
        return jnp.concatenate(rs, axis=0)

    def activate(partials):
        score = jnp.sum(partials, axis=-1, keepdims=True)
        return jnp.broadcast_to(jax.nn.gelu(score), partials.shape)

    def store_act(a_ref, act):
        a_ref[...] = act

    def update(a_ref, s0, width, carry):
        prev_tok, acc_lo, acc_hi = carry
        toks, gss = decode(s0, width)
        for j in range(width):
            lo, hi = [], []
            for k in range(PEER_G):
                pair = j * PEER_G + k
                a = a_ref[pair:pair + 1, :] * g_sms[k][gss[j]]
                v = v_ref[e_sms[k][gss[j]]].astype(F32)
                lo.append(a * v[0:SUBLANES])
                hi.append(a * v[SUBLANES:])
            new = toks[j] != prev_tok
            acc_lo = jnp.where(new, acc_ref[toks[j], 0:SUBLANES, :], acc_lo) + tree_sum(lo)
            acc_hi = jnp.where(new, acc_ref[toks[j], SUBLANES:, :], acc_hi) + tree_sum(hi)
            o_ref[toks[j], 0:SUBLANES, :] = acc_lo
            o_ref[toks[j], SUBLANES:, :] = acc_hi
            prev_tok = toks[j]
        return prev_tok, acc_lo, acc_hi

    w = PEER_UNROLL
    nb = n // w
    last = jnp.maximum(nb - 1, 0)
    zero = jnp.zeros((SUBLANES, LANES), F32)
    store_act(a0_ref, activate(scores(0, w)))
    partials = scores(jnp.minimum(1, last) * w, w)

    def step(i, a_store, a_load, partials, cr):
        act = activate(partials)
        partials = scores(jnp.minimum(i, last) * w, w)
        cr = update(a_load, (i - 2) * w, w, cr)
        store_act(a_store, act)
        return partials, cr

    def two_steps(h, state):
        partials, cr = state[0], state[1:]
        i = 2 * h + 2
        partials, cr = step(i, a1_ref, a0_ref, partials, cr)
        partials, cr = step(i + 1, a0_ref, a1_ref, partials, cr)
        return (partials,) + cr

    lax.fori_loop(0, nb // 2, two_steps, (partials, jnp.int32(-1), zero, zero))


def _peer(e_flat, g_flat, rec, cnt, t3, u3, v3, acc3):
    t = t3.shape[0]
    tb = min(PEER_TB, t)
    kern = functools.partial(_peer_kernel, tb=tb)
    tok_spec = pl.BlockSpec((tb, ROW_TILES, LANES), lambda b, c, n: (b, 0, 0))
    tab_spec = pl.BlockSpec((PEER_EC, ROW_TILES, LANES), lambda b, c, n: (c[0], 0, 0),
                            pipeline_mode=pl.Buffered(1))
    call = pl.pallas_call(
        kern,
        grid_spec=pltpu.PrefetchScalarGridSpec(
            num_scalar_prefetch=2,
            grid=(t // tb,),
            in_specs=[pl.BlockSpec(memory_space=pl.ANY), pl.BlockSpec(memory_space=pl.ANY),
                      pl.BlockSpec(memory_space=pl.ANY), tok_spec, tab_spec, tab_spec, tok_spec],
            out_specs=tok_spec,
            scratch_shapes=[
                pltpu.SMEM((2 * rec.shape[1],), jnp.int32),
                pltpu.SemaphoreType.DMA((2, 1 + 2 * PEER_G)),
                pltpu.VMEM((PEER_UNROLL * PEER_G, LANES), F32),
                pltpu.VMEM((PEER_UNROLL * PEER_G, LANES), F32),
            ] + [pltpu.SMEM((2 * tb * PEER_TGROUPS,), jnp.int32)] * PEER_G
              + [pltpu.SMEM((2 * tb * PEER_TGROUPS,), F32)] * PEER_G
              + [pltpu.VMEM((PEER_G * SUBLANES, LANES), F32)] * PEER_UNROLL,
        ),
        out_shape=jax.ShapeDtypeStruct(acc3.shape, F32),
        input_output_aliases={8: 0},
        compiler_params=_cparams(("arbitrary",)),
        name="peer",
    )
    acc = acc3
    for c in range(PEER_NCHUNK):
        acc = call(jnp.full((1,), c, jnp.int32), cnt, e_flat, g_flat, rec, t3, u3, v3, acc)
    return acc


def _final_kernel(a_ref, g_ref, o_ref):
    a = a_ref[...].reshape(o_ref.shape)
    ms = jnp.mean(a * a, axis=-1, keepdims=True)
    o_ref[...] = a * lax.rsqrt(ms + EPS) * g_ref[...]


def _final(acc2, g):
    t = acc2.shape[0]
    tm = min(FINAL_TM, t)
    return pl.pallas_call(
        _final_kernel,
        grid=(t // tm,),
        in_specs=[pl.BlockSpec((tm, ROW_TILES, LANES), lambda i: (i, 0, 0)),
                  pl.BlockSpec((1, D_MODEL), lambda i: (0, 0))],
        out_specs=pl.BlockSpec((tm, D_MODEL), lambda i: (i, 0)),
        out_shape=jax.ShapeDtypeStruct((t, D_MODEL), F32),
        compiler_params=_cparams(("arbitrary",)),
        name="final",
    )(acc2, g)


def _layer(x, norm1_g, w_in, lam_re, lam_im, log_dt, b_re, b_im, c_re, c_im, d_skip, w_glu, conv_w, conv_b,
           w_q, w_k, w_v, b_i, b_f, mh_gain, mlstm_skip, w_mlstm_out, w_out, norm2_g, w_query, key1, key2,
           expert_u, expert_v):
    nb, seq, d = x.shape
    t = nb * seq
    x2 = x.reshape(t, d)
    row = lambda a: a.reshape(1, -1).astype(F32)

    o_ssm, o_xm, o_z, o_i, o_f, o_ga, o_gb = 0, 512, 2048, 3584, 3588, 3592, 5640
    zeros = lambda n: jnp.zeros((d, n), w_in.dtype)
    w_cat = jnp.concatenate([
        w_in[:, o_xm:o_z], w_in[:, o_z:o_i], w_in[:, o_ssm:o_xm], w_in[:, o_i:o_ga],
        zeros(COL_GA - COL_GATE - 2 * HEADS), w_in[:, o_ga:o_gb], w_in[:, o_gb:]], axis=1).astype(BF16)
    proj = _inproj(x2, row(norm1_g), w_cat)
    proj3 = proj.reshape(nb, seq, PROJ_W)

    bblk, a_re, a_im, cblk = _s5_params(lam_re, lam_im, log_dt, b_re, b_im, c_re, c_im)
    y_a = _s5(proj3, bblk, a_re, a_im, cblk, row(d_skip), w_glu.astype(BF16))

    gate_bias = jnp.concatenate([b_i.astype(F32), b_f.astype(F32), jnp.zeros((LANES - 2 * HEADS,), F32)]).reshape(1, LANES)
    y_b = _mlstm(proj3, conv_w.astype(F32), row(conv_b), w_q.astype(BF16), w_k.astype(BF16), w_v.astype(BF16),
                 gate_bias, row(mh_gain), row(mlstm_skip), w_mlstm_out.astype(BF16))

    h1, hn2, qr = _merge(proj, y_a.reshape(t, d), y_b.reshape(t, d), x2, w_out.astype(BF16), row(norm2_g),
                         w_query.astype(BF16))

    e_t, g_t, rec, cnt8 = _route(qr, key1.astype(BF16), key2.astype(BF16))
    tb = min(PEER_TB, t)
    by_pair = lambda a: a.reshape(PEER_G, PEER_TGROUPS, t // tb, tb).transpose(0, 2, 1, 3).reshape(PEER_G, -1)
    e_flat = by_pair(e_t)
    g_flat = by_pair(g_t)
    cnt = cnt8[:, 0].reshape(-1, SUBLANES)[:, :PEER_NCHUNK].reshape(-1)

    dense = lambda a: a.reshape(a.shape[0], ROW_TILES, LANES)
    return _peer(e_flat, g_flat, rec, cnt, hn2, dense(expert_u.astype(BF16)), dense(expert_v.astype(BF16)), h1)


def kernel(x, norm1_g, w_in, lam_re, lam_im, log_dt, b_re, b_im, c_re, c_im, d_skip, w_glu, conv_w, conv_b, w_q, w_k, w_v, b_i, b_f, mh_gain, mlstm_skip, w_mlstm_out, w_out, norm2_g, w_query, key1, key2, expert_u, expert_v, norm_f_g):
    depth = w_in.shape[0]
    nb, seq, d = x.shape
    h = x
    for l in range(depth):
        acc = _layer(h, norm1_g[l], w_in[l], lam_re[l], lam_im[l], log_dt[l], b_re[l], b_im[l], c_re[l], c_im[l],
                     d_skip[l], w_glu[l], conv_w[l], conv_b[l], w_q[l], w_k[l], w_v[l], b_i[l], b_f[l], mh_gain[l],
                     mlstm_skip[l], w_mlstm_out[l], w_out[l], norm2_g[l], w_query[l], key1[l], key2[l],
                     expert_u[l], expert_v[l])
        h = acc.reshape(nb, seq, d) if l + 1 < depth else None
    out = _final(acc, norm_f_g.reshape(1, d).astype(F32))
    return out.reshape(nb, seq, d)
```

```python
import functools
import math

import jax
import jax.numpy as jnp
import numpy as np
from jax import lax
from jax.experimental import pallas as pl
from jax.experimental.pallas import tpu as pltpu

F32 = jnp.float32
BF16 = jnp.bfloat16
EPS = 1e-6

D_MODEL = 2048
SSM_WIDTH = 512
SSM_GROUP = 16
SSM_GROUPS = 32
SSM_STATE = 64
N_STATE = SSM_GROUPS * SSM_STATE
MLSTM_WIDTH = 1536
HEADS = 4
HEAD_DIM = 384
CONV_WIDTH = 4
CHUNK = 64
PEER_HEADS = 8
PEER_HALF = 128
PEER_NKEYS = 128
PEER_TOPK = 16
PEER_EXPERTS = PEER_NKEYS * PEER_NKEYS
HK = PEER_HEADS * PEER_TOPK

LANES = 128
SUBLANES = 8
ROW_TILES = D_MODEL // LANES

COL_XM = 0
COL_Z = 1536
COL_SSM = 3072
COL_GATE = 3584
COL_GA = 4096
COL_GB = 6144
PROJ_W = 8192

INPROJ_TM = 1024
INPROJ_TN = 1024
S5_LT = 32
MLSTM_TT = 256
MERGE_TM = 256
ROUTE_ST = 512
PEER_EC = 4096
PEER_NCHUNK = PEER_EXPERTS // PEER_EC
PEER_SHIFT = int(math.log2(PEER_EC))
PEER_TB = 128
PEER_G = 8
PEER_TGROUPS = 20
PEER_SLOTS = PEER_TGROUPS * PEER_G
PEER_UNROLL = 4
PEER_LIST_ALIGN = 2 * PEER_UNROLL
FINAL_TM = 512
assert PEER_G == SUBLANES and (HK + PEER_NCHUNK * (PEER_G - 1)) // PEER_G <= PEER_TGROUPS - 1
assert 2 * PEER_NCHUNK <= SUBLANES

VMEM_LIMIT = 56 * 1024 * 1024


def _cparams(sem):
    return pltpu.CompilerParams(dimension_semantics=sem, vmem_limit_bytes=VMEM_LIMIT)


def _const_spec(shape):
    nd = len(shape)
    return pl.BlockSpec(shape, lambda *_: (0,) * nd, pipeline_mode=pl.Buffered(1))


def _inproj_kernel(x_ref, g_ref, w_ref, o_ref, hn_ref):
    @pl.when(pl.program_id(1) == 0)
    def _():
        x = x_ref[...]
        y = x * lax.rsqrt(jnp.mean(x * x, axis=-1, keepdims=True) + EPS) * g_ref[...]
        hn_ref[...] = y.astype(BF16)

    o_ref[...] = jnp.dot(hn_ref[...], w_ref[...], preferred_element_type=F32)


def _inproj(x2, g, w_cat):
    t = x2.shape[0]
    tm = min(INPROJ_TM, t)
    return pl.pallas_call(
        _inproj_kernel,
        grid=(t // tm, PROJ_W // INPROJ_TN),
        in_specs=[
            pl.BlockSpec((tm, D_MODEL), lambda i, n: (i, 0)),
            pl.BlockSpec((1, D_MODEL), lambda i, n: (0, 0)),
            pl.BlockSpec((D_MODEL, INPROJ_TN), lambda i, n: (0, n)),
        ],
        out_specs=pl.BlockSpec((tm, INPROJ_TN), lambda i, n: (i, n)),
        out_shape=jax.ShapeDtypeStruct((t, PROJ_W), F32),
        scratch_shapes=[pltpu.VMEM((tm, D_MODEL), BF16)],
        compiler_params=_cparams(("arbitrary", "arbitrary")),
        name="inproj",
    )(x2, g, w_cat)


def _s5_kernel(u_ref, perm_ref, permt_ref, bblk_ref, are_ref, aim_ref, cblk_ref, d_ref, wglu_ref, o_ref,
               x_ref, st_ref, *, nb, lt):
    @pl.when(pl.program_id(0) == 0)
    def _():
        st_ref[...] = jnp.zeros_like(st_ref)

    rows = nb * lt
    u = u_ref[...].reshape(rows, SSM_WIDTH)
    ut = jnp.dot(perm_ref[...], u.astype(BF16), preferred_element_type=F32).astype(BF16)
    bu = jnp.dot(ut, bblk_ref[...], preferred_element_type=F32)
    n_tiles = N_STATE // LANES
    for j in range(2 * n_tiles):
        x_ref[j] = bu[:, j * LANES:(j + 1) * LANES]

    tiles_per_pass = 4
    for t0 in range(0, n_tiles, tiles_per_pass):
        tiles = range(t0, t0 + tiles_per_pass)
        a_re = [are_ref[:, j * LANES:(j + 1) * LANES] for j in tiles]
        a_im = [aim_ref[:, j * LANES:(j + 1) * LANES] for j in tiles]

        def step(s, carry):
            r = pl.ds(pl.multiple_of(s * nb, nb), nb)
            out = []
            for i, j in enumerate(tiles):
                s_re, s_im = carry[2 * i], carry[2 * i + 1]
                n_re = a_re[i] * s_re - a_im[i] * s_im + x_ref[j, r, :]
                n_im = a_re[i] * s_im + a_im[i] * s_re + x_ref[n_tiles + j, r, :]
                x_ref[j, r, :] = n_re
                x_ref[n_tiles + j, r, :] = n_im
                out += [n_re, n_im]
            return tuple(out)

        init = []
        for j in tiles:
            init += [st_ref[j], st_ref[n_tiles + j]]
        fin = lax.fori_loop(0, lt, step, tuple(init))
        for i, j in enumerate(tiles):
            st_ref[j] = fin[2 * i]
            st_ref[n_tiles + j] = fin[2 * i + 1]

    xs = jnp.concatenate([x_ref[j].astype(BF16) for j in range(2 * n_tiles)], axis=-1)
    z = jnp.dot(xs, cblk_ref[...], preferred_element_type=F32)
    z = jnp.dot(permt_ref[...], z, preferred_element_type=F32, precision=lax.Precision.HIGHEST)
    y = z + d_ref[...] * u
    y = jax.nn.gelu(y)
    vg = jnp.dot(y.astype(BF16), wglu_ref[...], preferred_element_type=F32)
    out = vg[:, :D_MODEL] * jax.nn.sigmoid(vg[:, D_MODEL:])
    o_ref[...] = out.reshape(nb, lt, D_MODEL)


def _s5(proj3, bblk, a_re, a_im, cblk, d_skip, w_glu):
    nb, seq, _ = proj3.shape
    lt = min(S5_LT, seq)
    kern = functools.partial(_s5_kernel, nb=nb, lt=lt)
    src = (jnp.arange(nb)[None, :] * lt + jnp.arange(lt)[:, None]).reshape(-1)
    perm = jax.nn.one_hot(src, nb * lt, dtype=F32)
    return pl.pallas_call(
        kern,
        grid=(seq // lt,),
        in_specs=[
            pl.BlockSpec((nb, lt, SSM_WIDTH), lambda c: (0, c, COL_SSM // SSM_WIDTH)),
            _const_spec((nb * lt, nb * lt)),
            _const_spec((nb * lt, nb * lt)),
            _const_spec((SSM_WIDTH, 2 * N_STATE)),
            _const_spec((1, N_STATE)),
            _const_spec((1, N_STATE)),
            _const_spec((2 * N_STATE, SSM_WIDTH)),
            _const_spec((1, SSM_WIDTH)),
            _const_spec((SSM_WIDTH, 2 * D_MODEL)),
        ],
        out_specs=pl.BlockSpec((nb, lt, D_MODEL), lambda c: (0, c, 0)),
        out_shape=jax.ShapeDtypeStruct((nb, seq, D_MODEL), F32),
        scratch_shapes=[pltpu.VMEM((2 * N_STATE // LANES, nb * lt, LANES), F32),
                        pltpu.VMEM((2 * N_STATE // LANES, nb, LANES), F32)],
        compiler_params=_cparams(("arbitrary",)),
        name="s5",
    )(proj3, perm.astype(BF16), perm.T, bblk, a_re, a_im, cblk, d_skip, w_glu)


def _s5_params(lam_re, lam_im, log_dt, b_re, b_im, c_re, c_im):
    g, p, h = SSM_GROUPS, SSM_STATE, SSM_GROUP
    lam = lax.complex(lam_re.astype(F32), lam_im.astype(F32))
    dt = jnp.exp(log_dt.astype(F32))[:, None]
    a_bar = jnp.exp(lam * dt)
    b_bar = ((a_bar - 1.0) / lam)[..., None] * lax.complex(b_re.astype(F32), b_im.astype(F32))
    eye = jnp.eye(g, dtype=F32)
    bb_re = jnp.einsum("gph,gk->ghkp", jnp.real(b_bar), eye).reshape(g * h, g * p)
    bb_im = jnp.einsum("gph,gk->ghkp", jnp.imag(b_bar), eye).reshape(g * h, g * p)
    bblk = jnp.concatenate([bb_re, bb_im], axis=1).astype(BF16)
    cc_re = jnp.einsum("ghp,gk->kpgh", c_re.astype(F32), eye).reshape(g * p, g * h)
    cc_im = jnp.einsum("ghp,gk->kpgh", c_im.astype(F32), eye).reshape(g * p, g * h)
    cblk = jnp.concatenate([cc_re, -cc_im], axis=0).astype(BF16)
    return bblk, jnp.real(a_bar).reshape(1, g * p), jnp.imag(a_bar).reshape(1, g * p), cblk


def _mlstm_kernel(xm_ref, z_ref, gt_ref, cw_ref, cb_ref, wq_ref, wk_ref, wv_ref, gb_ref, gain_ref, skip_ref,
                  wp_ref, o_ref, cbuf_ref, c_ref, n_ref, m_ref, h_ref, *, tt):
    halo = SUBLANES

    @pl.when(pl.program_id(1) == 0)
    def _():
        cbuf_ref[0:halo, :] = jnp.zeros((halo, MLSTM_WIDTH), F32)
        c_ref[...] = jnp.zeros_like(c_ref)
        n_ref[...] = jnp.zeros_like(n_ref)
        m_ref[...] = jnp.zeros_like(m_ref)

    @pl.when(pl.program_id(1) > 0)
    def _():
        cbuf_ref[0:halo, :] = cbuf_ref[tt:tt + halo, :]

    xm = xm_ref[0]
    cbuf_ref[halo:halo + tt, :] = xm
    conv = cb_ref[...]
    for j in range(CONV_WIDTH):
        off = halo - (CONV_WIDTH - 1) + j
        conv = conv + cbuf_ref[off:off + tt, :] * cw_ref[j:j + 1, :]
    xc = conv * jax.nn.sigmoid(conv)
    xcb = xc.astype(BF16)
    xmb = xm.astype(BF16)

    row_i = lax.broadcasted_iota(jnp.int32, (CHUNK, CHUNK), 0)
    col_i = lax.broadcasted_iota(jnp.int32, (CHUNK, CHUNK), 1)
    causal = col_i <= row_i
    tril = causal.astype(F32)
    triu = (row_i <= col_i).astype(F32)

    qs, ks, vs = [], [], []
    for h in range(HEADS):
        hs = slice(h * HEAD_DIM, (h + 1) * HEAD_DIM)
        qs.append(jnp.dot(xcb[:, hs], wq_ref[h], preferred_element_type=F32))
        ks.append(jnp.dot(xcb[:, hs], wk_ref[h], preferred_element_type=F32) * (HEAD_DIM ** -0.5))
        vs.append(jnp.dot(xmb[:, hs], wv_ref[h], preferred_element_type=F32))

    for j in range(tt // CHUNK):
        rs = slice(j * CHUNK, (j + 1) * CHUNK)
        gcol = gt_ref[0, rs, :] + gb_ref[...]
        lf_col = jax.nn.log_sigmoid(gcol)
        bcum_col = jnp.dot(tril, lf_col, preferred_element_type=F32, precision=lax.Precision.HIGHEST)
        grow = gcol.T
        lf_row = jax.nn.log_sigmoid(grow[0:SUBLANES, :])
        bcum_row = jnp.dot(lf_row, triu, preferred_element_type=F32, precision=lax.Precision.HIGHEST)
        for h in range(HEADS):
            hs = slice(h * HEAD_DIM, (h + 1) * HEAD_DIM)
            qc, kc, vc = qs[h][rs], ks[h][rs], vs[h][rs]
            bc = bcum_col[:, HEADS + h:HEADS + h + 1]
            br = bcum_row[HEADS + h:HEADS + h + 1, :]
            ic = gcol[:, h:h + 1]
            ir = grow[h:h + 1, :]
            m_prev = m_ref[h:h + 1, 0:1]
            log_w = jnp.where(causal, bc - br + ir, -jnp.inf)
            log_inter = bc + m_prev
            m_t = jnp.maximum(log_inter, jnp.max(log_w, axis=-1, keepdims=True))
            w = jnp.exp(log_w - m_t)
            inter = jnp.exp(log_inter - m_t)
            qcb, kcb, vcb = qc.astype(BF16), kc.astype(BF16), vc.astype(BF16)
            s = lax.dot_general(qcb, kcb, (((1,), (1,)), ((), ())), preferred_element_type=F32)
            sw = s * w
            c_prev = c_ref[h]
            n_prev = n_ref[h:h + 1, :]
            qcmem = lax.dot_general(qcb, c_prev.astype(BF16), (((1,), (1,)), ((), ())),
                                    preferred_element_type=F32)
            num = jnp.dot(sw.astype(BF16), vcb, preferred_element_type=F32) + inter * qcmem
            den = jnp.sum(sw, axis=-1, keepdims=True) + inter * jnp.sum(qc * n_prev, axis=-1, keepdims=True)
            hh = num / jnp.maximum(jnp.abs(den), jnp.exp(-m_t))
            h_ref[rs, hs] = hh
            b_last = bc[CHUNK - 1:CHUNK, :]
            m_new = m_t[CHUNK - 1:CHUNK, :]
            w_end = jnp.exp(b_last - bc + ic - m_new)
            decay = jnp.exp(b_last + m_prev - m_new)
            vw = (vc * w_end).astype(BF16)
            c_ref[h] = decay * c_prev + lax.dot_general(vw, kcb, (((0,), (0,)), ((), ())),
                                                        preferred_element_type=F32)
            n_ref[h:h + 1, :] = decay * n_prev + jnp.sum(w_end * kc, axis=0, keepdims=True)
            m_ref[h:h + 1, :] = jnp.broadcast_to(m_new, (1, LANES))

    z = z_ref[0]
    outs = []
    for h in range(HEADS):
        hs = slice(h * HEAD_DIM, (h + 1) * HEAD_DIM)
        hg = jax.nn.sigmoid(z[:, hs]) * h_ref[:, hs]
        mu = jnp.mean(hg, axis=-1, keepdims=True)
        dv = hg - mu
        var = jnp.mean(dv * dv, axis=-1, keepdims=True)
        outs.append(dv * lax.rsqrt(var + EPS))
    hn = jnp.concatenate(outs, axis=-1)
    hn = hn * gain_ref[...] + skip_ref[...] * xc
    o_ref[0] = jnp.dot(hn.astype(BF16), wp_ref[...], preferred_element_type=F32)


def _mlstm(proj3, conv_w, conv_b, w_q, w_k, w_v, gate_bias, gain, skip, w_proj):
    nb, seq, _ = proj3.shape
    tt = min(MLSTM_TT, seq)
    kern = functools.partial(_mlstm_kernel, tt=tt)
    return pl.pallas_call(
        kern,
        grid=(nb, seq // tt),
        in_specs=[
            pl.BlockSpec((1, tt, MLSTM_WIDTH), lambda b, t: (b, t, COL_XM // MLSTM_WIDTH)),
            pl.BlockSpec((1, tt, MLSTM_WIDTH), lambda b, t: (b, t, COL_Z // MLSTM_WIDTH)),
            pl.BlockSpec((1, tt, LANES), lambda b, t: (b, t, COL_GATE // LANES)),
            _const_spec((CONV_WIDTH, MLSTM_WIDTH)),
            _const_spec((1, MLSTM_WIDTH)),
            _const_spec((HEADS, HEAD_DIM, HEAD_DIM)),
            _const_spec((HEADS, HEAD_DIM, HEAD_DIM)),
            _const_spec((HEADS, HEAD_DIM, HEAD_DIM)),
            _const_spec((1, LANES)),
            _const_spec((1, MLSTM_WIDTH)),
            _const_spec((1, MLSTM_WIDTH)),
            _const_spec((MLSTM_WIDTH, D_MODEL)),
        ],
        out_specs=pl.BlockSpec((1, tt, D_MODEL), lambda b, t: (b, t, 0)),
        out_shape=jax.ShapeDtypeStruct((nb, seq, D_MODEL), F32),
        scratch_shapes=[
            pltpu.VMEM((tt + SUBLANES, MLSTM_WIDTH), F32),
            pltpu.VMEM((HEADS, HEAD_DIM, HEAD_DIM), F32),
            pltpu.VMEM((SUBLANES, HEAD_DIM), F32),
            pltpu.VMEM((SUBLANES, LANES), F32),
            pltpu.VMEM((tt, MLSTM_WIDTH), F32),
        ],
        compiler_params=_cparams(("arbitrary", "arbitrary")),
        name="mlstm",
    )(proj3, proj3, proj3, conv_w, conv_b, w_q, w_k, w_v, gate_bias, gain, skip, w_proj)


def _merge_kernel(ga_ref, gb_ref, ya_ref, yb_ref, x_ref, wo_ref, g2_ref, wq_ref, h1_ref, hn_ref, q_ref):
    merged = jax.nn.sigmoid(ga_ref[...]) * ya_ref[...] + jax.nn.sigmoid(gb_ref[...]) * yb_ref[...]
    h1 = x_ref[...] + jnp.dot(merged.astype(BF16), wo_ref[...], preferred_element_type=F32)
    h1_ref[...] = h1.reshape(h1_ref.shape)
    hn = h1 * lax.rsqrt(jnp.mean(h1 * h1, axis=-1, keepdims=True) + EPS) * g2_ref[...]
    hn_ref[...] = hn.reshape(hn_ref.shape)
    q_ref[...] = jnp.dot(hn.astype(BF16), wq_ref[...], preferred_element_type=F32)


def _merge(proj, y_a, y_b, x2, w_out, g2, w_query):
    t = x2.shape[0]
    tm = min(MERGE_TM, t)
    row = lambda c: pl.BlockSpec((tm, D_MODEL), lambda i: (i, c))
    out_sd = jax.ShapeDtypeStruct((t, D_MODEL), F32)
    dense = pl.BlockSpec((tm, ROW_TILES, LANES), lambda i: (i, 0, 0))
    dense_sd = jax.ShapeDtypeStruct((t, ROW_TILES, LANES), F32)
    return pl.pallas_call(
        _merge_kernel,
        grid=(t // tm,),
        in_specs=[row(COL_GA // D_MODEL), row(COL_GB // D_MODEL), row(0), row(0), row(0),
                  _const_spec((D_MODEL, D_MODEL)), _const_spec((1, D_MODEL)), _const_spec((D_MODEL, D_MODEL))],
        out_specs=[dense, dense, row(0)],
        out_shape=[dense_sd, dense_sd, out_sd],
        compiler_params=_cparams(("arbitrary",)),
        name="merge",
    )(proj, proj, y_a, y_b, x2, w_out, g2, w_query)


_CAND_ROWS = [PEER_TOPK // (i + 1) for i in range(PEER_TOPK)]
_CAND_OFFS = [int(v) for v in np.cumsum([0] + _CAND_ROWS[:-1])]
_NCAND = int(sum(_CAND_ROWS))
_NCAND_PAD = 56


def _extract_topk(s, payload, k, val_ref, pay_ref):
    n = s.shape[0]
    big = jnp.int32(2 ** 30)
    for r in range(k):
        m = jnp.max(s, axis=0, keepdims=True)
        sel = jnp.min(jnp.where(s == m, payload, big), axis=0, keepdims=True)
        val_ref[r:r + 1, :] = m
        pay_ref[r:r + 1, :] = sel
        s = jnp.where(payload == sel, -jnp.inf, s)


def _route_kernel(q_ref, k1_ref, k2_ref, e_ref, g_ref, rec_ref, cnt_ref,
                  v1_ref, i1_ref, v2_ref, i2_ref, cv_ref, ce_ref, cp_ref, tv_ref, tp_ref, ea_ref, gate_ref,
                  rows_ref, lhs_ref, *, tb):
    key_iota = lax.broadcasted_iota(jnp.int32, (PEER_NKEYS, tb), 0)
    cand_iota = lax.broadcasted_iota(jnp.int32, (_NCAND_PAD, tb), 0)
    for h in range(PEER_HEADS):
        q1 = q_ref[:, h * 2 * PEER_HALF:h * 2 * PEER_HALF + PEER_HALF].astype(BF16)
        q2 = q_ref[:, h * 2 * PEER_HALF + PEER_HALF:(h + 1) * 2 * PEER_HALF].astype(BF16)
        s1 = lax.dot_general(k1_ref[h], q1, (((1,), (1,)), ((), ())), preferred_element_type=F32)
        s2 = lax.dot_general(k2_ref[h], q2, (((1,), (1,)), ((), ())), preferred_element_type=F32)
        _extract_topk(s1, key_iota, PEER_TOPK, v1_ref, i1_ref)
        _extract_topk(s2, key_iota, PEER_TOPK, v2_ref, i2_ref)
        cv_ref[...] = jnp.full((_NCAND_PAD, tb), -jnp.inf, F32)
        ce_ref[...] = jnp.zeros((_NCAND_PAD, tb), jnp.int32)
        cp_ref[...] = cand_iota + jnp.int32(1 << 20)
        for i in range(PEER_TOPK):
            n_i, off = _CAND_ROWS[i], _CAND_OFFS[i]
            cv_ref[off:off + n_i, :] = v1_ref[i:i + 1, :] + v2_ref[0:n_i, :]
            ce_ref[off:off + n_i, :] = i1_ref[i:i + 1, :] * PEER_NKEYS + i2_ref[0:n_i, :]
            cp_ref[off:off + n_i, :] = i * PEER_TOPK + lax.broadcasted_iota(jnp.int32, (n_i, tb), 0)
        cv = cv_ref[...]
        cp = cp_ref[...]
        _extract_topk(cv, cp, PEER_TOPK, tv_ref, tp_ref)
        ce = ce_ref[...]
        tv = tv_ref[...]
        ex = jnp.exp(tv - tv[0:1, :])
        gate_ref[h * PEER_TOPK:(h + 1) * PEER_TOPK, :] = ex / jnp.sum(ex, axis=0, keepdims=True)
        for r in range(PEER_TOPK):
            ea_ref[h * PEER_TOPK + r:h * PEER_TOPK + r + 1, :] = jnp.sum(
                jnp.where(cp == tp_ref[r:r + 1, :], ce, 0), axis=0, keepdims=True)

    e_all = ea_ref[...]
    g_all = gate_ref[...]
    chunk = e_all >> PEER_SHIFT
    e_loc = e_all & (PEER_EC - 1)
    ri = lax.broadcasted_iota(jnp.int32, (HK, HK), 0)
    ci = lax.broadcasted_iota(jnp.int32, (HK, HK), 1)
    lower = (ci < ri).astype(BF16)
    dest = jnp.zeros((HK, tb), F32)
    gstart = jnp.zeros((1, tb), F32)
    rows_ref[...] = jnp.zeros(rows_ref.shape, F32)
    for c in range(PEER_NCHUNK):
        mask = chunk == c
        maskf = mask.astype(F32)
        rank = jnp.dot(lower, mask.astype(BF16), preferred_element_type=F32)
        grp = jnp.floor(rank * (1.0 / PEER_G))
        dest = dest + maskf * ((rank - grp * PEER_G) * PEER_TGROUPS + gstart + grp)
        cnt = jnp.sum(maskf, axis=0, keepdims=True).astype(jnp.int32)
        ng = jnp.maximum((cnt + (PEER_G - 1)) >> 3, 1).astype(F32)
        rows_ref[c:c + 1, :] = ng
        rows_ref[SUBLANES + c:SUBLANES + c + 1, :] = gstart
        gstart = gstart + ng
    dest_i = dest.astype(jnp.int32)
    for k in range(PEER_SLOTS):
        if k % PEER_TGROUPS == PEER_TGROUPS - 1:
            e_ref[k:k + 1, :] = jnp.zeros((1, tb), jnp.int32)
            g_ref[k:k + 1, :] = jnp.zeros((1, tb), F32)
            continue
        sel = dest_i == k
        e_ref[k:k + 1, :] = jnp.sum(jnp.where(sel, e_loc, 0), axis=0, keepdims=True)
        g_ref[k:k + 1, :] = jnp.sum(jnp.where(sel, g_all, 0.0), axis=0, keepdims=True)

    ng8 = rows_ref[0:SUBLANES, :]
    ti = lax.broadcasted_iota(jnp.int32, (tb, tb), 0)
    tj = lax.broadcasted_iota(jnp.int32, (tb, tb), 1)
    before = (ti < tj).astype(BF16)
    off8 = jnp.dot(ng8.astype(BF16), before, preferred_element_type=F32)
    rows_ref[2 * SUBLANES:3 * SUBLANES, :] = off8
    n8 = jnp.sum(ng8, axis=1, keepdims=True).astype(jnp.int32)
    npad8 = (-n8) & (PEER_LIST_ALIGN - 1)
    cnt_ref[...] = jnp.broadcast_to(n8 + npad8, (SUBLANES, LANES))
    cols = rows_ref[...].T
    tok_row = lax.broadcasted_iota(jnp.int32, (1, tb), 1).astype(F32)
    smax = tb * (HK // PEER_G)
    rec_ref[...] = jnp.zeros(rec_ref.shape, jnp.int32)
    for c in range(PEER_NCHUNK):
        off_row = off8[c:c + 1, :]
        off_hi = jnp.floor(off_row * (1.0 / 64.0))
        lhs_ref[...] = jnp.zeros(lhs_ref.shape, F32)
        lhs_ref[0:1, :] = tok_row
        lhs_ref[1:2, :] = rows_ref[SUBLANES + c:SUBLANES + c + 1, :]
        lhs_ref[2:3, :] = off_hi
        lhs_ref[3:4, :] = off_row - 64.0 * off_hi
        lhs = lhs_ref[...].astype(BF16)
        ng_col = cols[:, c:c + 1]
        off_col = cols[:, 2 * SUBLANES + c:2 * SUBLANES + c + 1]
        n_c = n8[c:c + 1, :].astype(F32)
        npad_c = npad8[c:c + 1, :].astype(F32)
        for j in range(smax // ROUTE_ST):
            s = (lax.broadcasted_iota(jnp.int32, (1, ROUTE_ST), 1) + j * ROUTE_ST).astype(F32) - npad_c
            onehot = jnp.logical_and(off_col <= s, s < off_col + ng_col).astype(BF16)
            r = jnp.dot(lhs, onehot, preferred_element_type=F32)
            tok = r[0:1, :]
            real = jnp.logical_and(s >= 0.0, s < n_c)
            grp = jnp.where(real, r[1:2, :] + s - (r[2:3, :] * 64.0 + r[3:4, :]), PEER_TGROUPS - 1.0)
            gslot = (grp * tb + tok).astype(jnp.int32) + (pl.program_id(0) & 1) * (tb * PEER_TGROUPS)
            tile = slice(j * ROUTE_ST, (j + 1) * ROUTE_ST)
            rec_ref[c:c + 1, tile] = tok.astype(jnp.int32)
            rec_ref[PEER_NCHUNK + c:PEER_NCHUNK + c + 1, tile] = gslot


def _route(qr, key1, key2):
    t = qr.shape[0]
    tb = min(PEER_TB, t)
    nblk = t // tb
    smax = tb * (HK // PEER_G)
    kern = functools.partial(_route_kernel, tb=tb)
    col = lambda r: pl.BlockSpec((r, tb), lambda i: (0, i))
    f_s = lambda r: pltpu.VMEM((r, tb), F32)
    i_s = lambda r: pltpu.VMEM((r, tb), jnp.int32)
    return pl.pallas_call(
        kern,
        grid=(nblk,),
        in_specs=[pl.BlockSpec((tb, D_MODEL), lambda i: (i, 0)),
                  _const_spec((PEER_HEADS, PEER_NKEYS, PEER_HALF)),
                  _const_spec((PEER_HEADS, PEER_NKEYS, PEER_HALF))],
        out_specs=[col(PEER_SLOTS), col(PEER_SLOTS),
                   pl.BlockSpec((SUBLANES, smax), lambda i: (i, 0)),
                   pl.BlockSpec((SUBLANES, LANES), lambda i: (i, 0))],
        out_shape=[jax.ShapeDtypeStruct((PEER_SLOTS, t), jnp.int32), jax.ShapeDtypeStruct((PEER_SLOTS, t), F32),
                   jax.ShapeDtypeStruct((nblk * SUBLANES, smax), jnp.int32),
                   jax.ShapeDtypeStruct((nblk * SUBLANES, LANES), jnp.int32)],
        scratch_shapes=[f_s(PEER_TOPK), i_s(PEER_TOPK), f_s(PEER_TOPK), i_s(PEER_TOPK),
                        f_s(_NCAND_PAD), i_s(_NCAND_PAD), i_s(_NCAND_PAD),
                        f_s(PEER_TOPK), i_s(PEER_TOPK), i_s(HK), f_s(HK),
                        f_s(LANES), f_s(SUBLANES)],
        compiler_params=_cparams(("arbitrary",)),
        name="route",
    )(qr, key1, key2)


def _peer_kernel(c_ref, cnt_ref, e_hbm, g_hbm, rec_hbm, t_ref, u_ref, v_ref, acc_ref, o_ref,
                 tok_sm, slot_sm, sems, a0_ref, a1_ref, *lists_and_partials, tb):
    c = c_ref[0]
    b = pl.program_id(0)
    ngs = tb * PEER_TGROUPS
    e_sms = lists_and_partials[0:PEER_G]
    g_sms = lists_and_partials[PEER_G:2 * PEER_G]
    p_refs = lists_and_partials[2 * PEER_G:]
    smax = rec_hbm.shape[1]
    par = b & 1

    def list_copies(blk, half):
        rec_dst = pl.ds(half * smax, smax)
        cps = [pltpu.make_async_copy(rec_hbm.at[blk * SUBLANES + c], tok_sm.at[rec_dst], sems.at[half, 0]),
               pltpu.make_async_copy(rec_hbm.at[blk * SUBLANES + PEER_NCHUNK + c], slot_sm.at[rec_dst],
                                     sems.at[half, 1])]
        for k in range(PEER_G):
            dst = pl.ds(half * ngs, ngs)
            cps.append(pltpu.make_async_copy(e_hbm.at[k, pl.ds(blk * ngs, ngs)], e_sms[k].at[dst],
                                             sems.at[half, 2 + k]))
            cps.append(pltpu.make_async_copy(g_hbm.at[k, pl.ds(blk * ngs, ngs)], g_sms[k].at[dst],
                                             sems.at[half, 2 + PEER_G + k]))
        return cps

    @pl.when(b == 0)
    def _():
        for cp in list_copies(b, par):
            cp.start()

    @pl.when(b + 1 < pl.num_programs(0))
    def _():
        for cp in list_copies(b + 1, 1 - par):
            cp.start()

    for cp in list_copies(b, par):
        cp.wait()
    n = cnt_ref[b * PEER_NCHUNK + c]
    rec0 = par * smax

    def tree_sum(xs):
        while len(xs) > 1:
            xs = [xs[i] + xs[i + 1] for i in range(0, len(xs) - 1, 2)] + ([xs[-1]] if len(xs) % 2 else [])
        return xs[0]

    w = PEER_UNROLL
    nb = n // w
    last = jnp.maximum(nb - 1, 0)

    def decode(batch):
        s0 = rec0 + batch * w
        return tuple(tok_sm[s0 + j] for j in range(w)), tuple(slot_sm[s0 + j] for j in range(w))

    def scores(groups):
        toks, gss = groups
        width = len(toks)
        for j in range(width):
            t_lo = t_ref[toks[j], 0:SUBLANES, :]
            t_hi = t_ref[toks[j], SUBLANES:, :]
            for k in range(PEER_G):
                u = u_ref[e_sms[k][gss[j]]].astype(F32)
                p_refs[j][k * SUBLANES:(k + 1) * SUBLANES, :] = t_lo * u[0:SUBLANES] + t_hi * u[SUBLANES:]
        rs = [tree_sum([p_refs[j][pl.ds(i, PEER_G, stride=SUBLANES), :] for i in range(SUBLANES)])
              for j in range(width)]
        return jnp.concatenate(rs, axis=0)

    def activate(partials):
        score = jnp.sum(partials, axis=-1, keepdims=True)
        return jnp.broadcast_to(jax.nn.gelu(score), partials.shape)

    def store_act(a_ref, act):
        a_ref[...] = act

    def update(a_ref, groups, carry):
        prev_tok, acc_lo, acc_hi = carry
        toks, gss = groups
        for j in range(len(toks)):
            lo, hi = [], []
            for k in range(PEER_G):
                pair = j * PEER_G + k
                a = a_ref[pair:pair + 1, :] * g_sms[k][gss[j]]
                v = v_ref[e_sms[k][gss[j]]].astype(F32)
                lo.append(a * v[0:SUBLANES])
                hi.append(a * v[SUBLANES:])
            new = toks[j] != prev_tok
            acc_lo = jnp.where(new, acc_ref[toks[j], 0:SUBLANES, :], acc_lo) + tree_sum(lo)
            acc_hi = jnp.where(new, acc_ref[toks[j], SUBLANES:, :], acc_hi) + tree_sum(hi)
            o_ref[toks[j], 0:SUBLANES, :] = acc_lo
            o_ref[toks[j], SUBLANES:, :] = acc_hi
            prev_tok = toks[j]
        return prev_tok, acc_lo, acc_hi

    zero = jnp.zeros((SUBLANES, LANES), F32)
    store_act(a0_ref, activate(scores(decode(0))))
    partials = scores(decode(jnp.minimum(1, last)))

    def step(i, a_store, a_load, partials, cr):
        act = activate(partials)
        partials = scores(decode(jnp.minimum(i, last)))
        cr = update(a_load, decode(i - 2), cr)
        store_act(a_store, act)
        return partials, cr

    def two_steps(h, state):
        partials, cr = state
        i = 2 * h + 2
        partials, cr = step(i, a1_ref, a0_ref, partials, cr)
        partials, cr = step(i + 1, a0_ref, a1_ref, partials, cr)
        return partials, cr

    lax.fori_loop(0, nb // 2, two_steps, (partials, (jnp.int32(-1), zero, zero)))


def _peer(e_flat, g_flat, rec, cnt, t3, u3, v3, acc3):
    t = t3.shape[0]
    tb = min(PEER_TB, t)
    kern = functools.partial(_peer_kernel, tb=tb)
    tok_spec = pl.BlockSpec((tb, ROW_TILES, LANES), lambda b, c, n: (b, 0, 0))
    tab_spec = pl.BlockSpec((PEER_EC, ROW_TILES, LANES), lambda b, c, n: (c[0], 0, 0),
                            pipeline_mode=pl.Buffered(1))
    call = pl.pallas_call(
        kern,
        grid_spec=pltpu.PrefetchScalarGridSpec(
            num_scalar_prefetch=2,
            grid=(t // tb,),
            in_specs=[pl.BlockSpec(memory_space=pl.ANY), pl.BlockSpec(memory_space=pl.ANY),
                      pl.BlockSpec(memory_space=pl.ANY), tok_spec, tab_spec, tab_spec, tok_spec],
            out_specs=tok_spec,
            scratch_shapes=[
                pltpu.SMEM((2 * rec.shape[1],), jnp.int32),
                pltpu.SMEM((2 * rec.shape[1],), jnp.int32),
                pltpu.SemaphoreType.DMA((2, 2 + 2 * PEER_G)),
                pltpu.VMEM((PEER_UNROLL * PEER_G, LANES), F32),
                pltpu.VMEM((PEER_UNROLL * PEER_G, LANES), F32),
            ] + [pltpu.SMEM((2 * tb * PEER_TGROUPS,), jnp.int32)] * PEER_G
              + [pltpu.SMEM((2 * tb * PEER_TGROUPS,), F32)] * PEER_G
              + [pltpu.VMEM((PEER_G * SUBLANES, LANES), F32)] * PEER_UNROLL,
        ),
        out_shape=jax.ShapeDtypeStruct(acc3.shape, F32),
        input_output_aliases={8: 0},
        compiler_params=_cparams(("arbitrary",)),
        name="peer",
    )
    acc = acc3
    for c in range(PEER_NCHUNK):
        acc = call(jnp.full((1,), c, jnp.int32), cnt, e_flat, g_flat, rec, t3, u3, v3, acc)
    return acc


def _final_kernel(a_ref, g_ref, o_ref):
    a = a_ref[...].reshape(o_ref.shape)
    ms = jnp.mean(a * a, axis=-1, keepdims=True)
    o_ref[...] = a * lax.rsqrt(ms + EPS) * g_ref[...]


def _final(acc2, g):
    t = acc2.shape[0]
    tm = min(FINAL_TM, t)
    return pl.pallas_call(
        _final_kernel,
        grid=(t // tm,),
        in_specs=[pl.BlockSpec((tm, ROW_TILES, LANES), lambda i: (i, 0, 0)),
                  pl.BlockSpec((1, D_MODEL), lambda i: (0, 0))],
        out_specs=pl.BlockSpec((tm, D_MODEL), lambda i: (i, 0)),
        out_shape=jax.ShapeDtypeStruct((t, D_MODEL), F32),
        compiler_params=_cparams(("arbitrary",)),
        name="final",
    )(acc2, g)


def _layer(x, norm1_g, w_in, lam_re, lam_im, log_dt, b_re, b_im, c_re, c_im, d_skip, w_glu, conv_w, conv_b,
           w_q, w_k, w_v, b_i, b_f, mh_gain, mlstm_skip, w_mlstm_out, w_out, norm2_g, w_query, key1, key2,
           expert_u, expert_v):
    nb, seq, d = x.shape
    t = nb * seq
    x2 = x.reshape(t, d)
    row = lambda a: a.reshape(1, -1).astype(F32)

    o_ssm, o_xm, o_z, o_i, o_f, o_ga, o_gb = 0, 512, 2048, 3584, 3588, 3592, 5640
    zeros = lambda n: jnp.zeros((d, n), w_in.dtype)
    w_cat = jnp.concatenate([
        w_in[:, o_xm:o_z], w_in[:, o_z:o_i], w_in[:, o_ssm:o_xm], w_in[:, o_i:o_ga],
        zeros(COL_GA - COL_GATE - 2 * HEADS), w_in[:, o_ga:o_gb], w_in[:, o_gb:]], axis=1).astype(BF16)
    proj = _inproj(x2, row(norm1_g), w_cat)
    proj3 = proj.reshape(nb, seq, PROJ_W)

    bblk, a_re, a_im, cblk = _s5_params(lam_re, lam_im, log_dt, b_re, b_im, c_re, c_im)
    y_a = _s5(proj3, bblk, a_re, a_im, cblk, row(d_skip), w_glu.astype(BF16))

    gate_bias = jnp.concatenate([b_i.astype(F32), b_f.astype(F32), jnp.zeros((LANES - 2 * HEADS,), F32)]).reshape(1, LANES)
    y_b = _mlstm(proj3, conv_w.astype(F32), row(conv_b), w_q.astype(BF16), w_k.astype(BF16), w_v.astype(BF16),
                 gate_bias, row(mh_gain), row(mlstm_skip), w_mlstm_out.astype(BF16))

    h1, hn2, qr = _merge(proj, y_a.reshape(t, d), y_b.reshape(t, d), x2, w_out.astype(BF16), row(norm2_g),
                         w_query.astype(BF16))

    e_t, g_t, rec, cnt8 = _route(qr, key1.astype(BF16), key2.astype(BF16))
    tb = min(PEER_TB, t)
    by_pair = lambda a: a.reshape(PEER_G, PEER_TGROUPS, t // tb, tb).transpose(0, 2, 1, 3).reshape(PEER_G, -1)
    e_flat = by_pair(e_t)
    g_flat = by_pair(g_t)
    cnt = cnt8[:, 0].reshape(-1, SUBLANES)[:, :PEER_NCHUNK].reshape(-1)

    dense = lambda a: a.reshape(a.shape[0], ROW_TILES, LANES)
    return _peer(e_flat, g_flat, rec, cnt, hn2, dense(expert_u.astype(BF16)), dense(expert_v.astype(BF16)), h1)


def kernel(x, norm1_g, w_in, lam_re, lam_im, log_dt, b_re, b_im, c_re, c_im, d_skip, w_glu, conv_w, conv_b, w_q, w_k, w_v, b_i, b_f, mh_gain, mlstm_skip, w_mlstm_out, w_out, norm2_g, w_query, key1, key2, expert_u, expert_v, norm_f_g):
    depth = w_in.shape[0]
    nb, seq, d = x.shape
    h = x
    for l in range(depth):
        acc = _layer(h, norm1_g[l], w_in[l], lam_re[l], lam_im[l], log_dt[l], b_re[l], b_im[l], c_re[l], c_im[l],
                     d_skip[l], w_glu[l], conv_w[l], conv_b[l], w_q[l], w_k[l], w_v[l], b_i[l], b_f[l], mh_gain[l],
                     mlstm_skip[l], w_mlstm_out[l], w_out[l], norm2_g[l], w_query[l], key1[l], key2[l],
                     expert_u[l], expert_v[l])
        h = acc.reshape(nb, seq, d) if l + 1 < depth else None
    out = _final(acc, norm_f_g.reshape(1, d).astype(F32))
    return out.reshape(nb, seq, d)
```

```python
import functools
import math

import jax
import jax.numpy as jnp
import numpy as np
from jax import lax
from jax.experimental import pallas as pl
from jax.experimental.pallas import tpu as pltpu

F32 = jnp.float32
BF16 = jnp.bfloat16
EPS = 1e-6

D_MODEL = 2048
SSM_WIDTH = 512
SSM_GROUP = 16
SSM_GROUPS = 32
SSM_STATE = 64
N_STATE = SSM_GROUPS * SSM_STATE
MLSTM_WIDTH = 1536
HEADS = 4
HEAD_DIM = 384
CONV_WIDTH = 4
CHUNK = 64
PEER_HEADS = 8
PEER_HALF = 128
PEER_NKEYS = 128
PEER_TOPK = 16
PEER_EXPERTS = PEER_NKEYS * PEER_NKEYS
HK = PEER_HEADS * PEER_TOPK

LANES = 128
SUBLANES = 8
ROW_TILES = D_MODEL // LANES

COL_XM = 0
COL_Z = 1536
COL_SSM = 3072
COL_GATE = 3584
COL_GA = 4096
COL_GB = 6144
PROJ_W = 8192

INPROJ_TM = 1024
INPROJ_TN = 1024
S5_LT = 32
MLSTM_TT = 256
MERGE_TM = 256
ROUTE_ST = 512
PEER_EC = 4096
PEER_NCHUNK = PEER_EXPERTS // PEER_EC
PEER_SHIFT = int(math.log2(PEER_EC))
PEER_TB = 128
PEER_G = 8
PEER_TGROUPS = 20
PEER_SLOTS = PEER_TGROUPS * PEER_G
PEER_UNROLL = 4
PEER_LIST_ALIGN = 2 * PEER_UNROLL
FINAL_TM = 512
assert PEER_G == SUBLANES and (HK + PEER_NCHUNK * (PEER_G - 1)) // PEER_G <= PEER_TGROUPS - 1
assert 2 * PEER_NCHUNK <= SUBLANES

VMEM_LIMIT = 56 * 1024 * 1024


def _cparams(sem):
    return pltpu.CompilerParams(dimension_semantics=sem, vmem_limit_bytes=VMEM_LIMIT)


def _const_spec(shape):
    nd = len(shape)
    return pl.BlockSpec(shape, lambda *_: (0,) * nd, pipeline_mode=pl.Buffered(1))


def _inproj_kernel(x_ref, g_ref, w_ref, o_ref, hn_ref):
    @pl.when(pl.program_id(1) == 0)
    def _():
        x = x_ref[...]
        y = x * lax.rsqrt(jnp.mean(x * x, axis=-1, keepdims=True) + EPS) * g_ref[...]
        hn_ref[...] = y.astype(BF16)

    o_ref[...] = jnp.dot(hn_ref[...], w_ref[...], preferred_element_type=F32)


def _inproj(x2, g, w_cat):
    t = x2.shape[0]
    tm = min(INPROJ_TM, t)
    return pl.pallas_call(
        _inproj_kernel,
        grid=(t // tm, PROJ_W // INPROJ_TN),
        in_specs=[
            pl.BlockSpec((tm, D_MODEL), lambda i, n: (i, 0)),
            pl.BlockSpec((1, D_MODEL), lambda i, n: (0, 0)),
            pl.BlockSpec((D_MODEL, INPROJ_TN), lambda i, n: (0, n)),
        ],
        out_specs=pl.BlockSpec((tm, INPROJ_TN), lambda i, n: (i, n)),
        out_shape=jax.ShapeDtypeStruct((t, PROJ_W), F32),
        scratch_shapes=[pltpu.VMEM((tm, D_MODEL), BF16)],
        compiler_params=_cparams(("arbitrary", "arbitrary")),
        name="inproj",
    )(x2, g, w_cat)


def _s5_kernel(u_ref, perm_ref, permt_ref, bblk_ref, are_ref, aim_ref, cblk_ref, d_ref, wglu_ref, o_ref,
               x_ref, st_ref, *, nb, lt):
    @pl.when(pl.program_id(0) == 0)
    def _():
        st_ref[...] = jnp.zeros_like(st_ref)

    rows = nb * lt
    u = u_ref[...].reshape(rows, SSM_WIDTH)
    ut = jnp.dot(perm_ref[...], u.astype(BF16), preferred_element_type=F32).astype(BF16)
    bu = jnp.dot(ut, bblk_ref[...], preferred_element_type=F32)
    n_tiles = N_STATE // LANES
    for j in range(2 * n_tiles):
        x_ref[j] = bu[:, j * LANES:(j + 1) * LANES]

    tiles_per_pass = 4
    for t0 in range(0, n_tiles, tiles_per_pass):
        tiles = range(t0, t0 + tiles_per_pass)
        a_re = [are_ref[:, j * LANES:(j + 1) * LANES] for j in tiles]
        a_im = [aim_ref[:, j * LANES:(j + 1) * LANES] for j in tiles]

        def step(s, carry):
            r = pl.ds(pl.multiple_of(s * nb, nb), nb)
            out = []
            for i, j in enumerate(tiles):
                s_re, s_im = carry[2 * i], carry[2 * i + 1]
                n_re = a_re[i] * s_re - a_im[i] * s_im + x_ref[j, r, :]
                n_im = a_re[i] * s_im + a_im[i] * s_re + x_ref[n_tiles + j, r, :]
                x_ref[j, r, :] = n_re
                x_ref[n_tiles + j, r, :] = n_im
                out += [n_re, n_im]
            return tuple(out)

        init = []
        for j in tiles:
            init += [st_ref[j], st_ref[n_tiles + j]]
        fin = lax.fori_loop(0, lt, step, tuple(init))
        for i, j in enumerate(tiles):
            st_ref[j] = fin[2 * i]
            st_ref[n_tiles + j] = fin[2 * i + 1]

    xs = jnp.concatenate([x_ref[j].astype(BF16) for j in range(2 * n_tiles)], axis=-1)
    z = jnp.dot(xs, cblk_ref[...], preferred_element_type=F32)
    z = jnp.dot(permt_ref[...], z, preferred_element_type=F32, precision=lax.Precision.HIGHEST)
    y = z + d_ref[...] * u
    y = jax.nn.gelu(y)
    vg = jnp.dot(y.astype(BF16), wglu_ref[...], preferred_element_type=F32)
    out = vg[:, :D_MODEL] * jax.nn.sigmoid(vg[:, D_MODEL:])
    o_ref[...] = out.reshape(nb, lt, D_MODEL)


def _s5(proj3, bblk, a_re, a_im, cblk, d_skip, w_glu):
    nb, seq, _ = proj3.shape
    lt = min(S5_LT, seq)
    kern = functools.partial(_s5_kernel, nb=nb, lt=lt)
    src = (jnp.arange(nb)[None, :] * lt + jnp.arange(lt)[:, None]).reshape(-1)
    perm = jax.nn.one_hot(src, nb * lt, dtype=F32)
    return pl.pallas_call(
        kern,
        grid=(seq // lt,),
        in_specs=[
            pl.BlockSpec((nb, lt, SSM_WIDTH), lambda c: (0, c, COL_SSM // SSM_WIDTH)),
            _const_spec((nb * lt, nb * lt)),
            _const_spec((nb * lt, nb * lt)),
            _const_spec((SSM_WIDTH, 2 * N_STATE)),
            _const_spec((1, N_STATE)),
            _const_spec((1, N_STATE)),
            _const_spec((2 * N_STATE, SSM_WIDTH)),
            _const_spec((1, SSM_WIDTH)),
            _const_spec((SSM_WIDTH, 2 * D_MODEL)),
        ],
        out_specs=pl.BlockSpec((nb, lt, D_MODEL), lambda c: (0, c, 0)),
        out_shape=jax.ShapeDtypeStruct((nb, seq, D_MODEL), F32),
        scratch_shapes=[pltpu.VMEM((2 * N_STATE // LANES, nb * lt, LANES), F32),
                        pltpu.VMEM((2 * N_STATE // LANES, nb, LANES), F32)],
        compiler_params=_cparams(("arbitrary",)),
        name="s5",
    )(proj3, perm.astype(BF16), perm.T, bblk, a_re, a_im, cblk, d_skip, w_glu)


def _s5_params(lam_re, lam_im, log_dt, b_re, b_im, c_re, c_im):
    g, p, h = SSM_GROUPS, SSM_STATE, SSM_GROUP
    lam = lax.complex(lam_re.astype(F32), lam_im.astype(F32))
    dt = jnp.exp(log_dt.astype(F32))[:, None]
    a_bar = jnp.exp(lam * dt)
    b_bar = ((a_bar - 1.0) / lam)[..., None] * lax.complex(b_re.astype(F32), b_im.astype(F32))
    eye = jnp.eye(g, dtype=F32)
    bb_re = jnp.einsum("gph,gk->ghkp", jnp.real(b_bar), eye).reshape(g * h, g * p)
    bb_im = jnp.einsum("gph,gk->ghkp", jnp.imag(b_bar), eye).reshape(g * h, g * p)
    bblk = jnp.concatenate([bb_re, bb_im], axis=1).astype(BF16)
    cc_re = jnp.einsum("ghp,gk->kpgh", c_re.astype(F32), eye).reshape(g * p, g * h)
    cc_im = jnp.einsum("ghp,gk->kpgh", c_im.astype(F32), eye).reshape(g * p, g * h)
    cblk = jnp.concatenate([cc_re, -cc_im], axis=0).astype(BF16)
    return bblk, jnp.real(a_bar).reshape(1, g * p), jnp.imag(a_bar).reshape(1, g * p), cblk


def _mlstm_kernel(xm_ref, z_ref, gt_ref, cw_ref, cb_ref, wq_ref, wk_ref, wv_ref, gb_ref, gain_ref, skip_ref,
                  wp_ref, o_ref, cbuf_ref, c_ref, n_ref, m_ref, h_ref, *, tt):
    halo = SUBLANES

    @pl.when(pl.program_id(1) == 0)
    def _():
        cbuf_ref[0:halo, :] = jnp.zeros((halo, MLSTM_WIDTH), F32)
        c_ref[...] = jnp.zeros_like(c_ref)
        n_ref[...] = jnp.zeros_like(n_ref)
        m_ref[...] = jnp.zeros_like(m_ref)

    @pl.when(pl.program_id(1) > 0)
    def _():
        cbuf_ref[0:halo, :] = cbuf_ref[tt:tt + halo, :]

    xm = xm_ref[0]
    cbuf_ref[halo:halo + tt, :] = xm
    conv = cb_ref[...]
    for j in range(CONV_WIDTH):
        off = halo - (CONV_WIDTH - 1) + j
        conv = conv + cbuf_ref[off:off + tt, :] * cw_ref[j:j + 1, :]
    xc = conv * jax.nn.sigmoid(conv)
    xcb = xc.astype(BF16)
    xmb = xm.astype(BF16)

    row_i = lax.broadcasted_iota(jnp.int32, (CHUNK, CHUNK), 0)
    col_i = lax.broadcasted_iota(jnp.int32, (CHUNK, CHUNK), 1)
    causal = col_i <= row_i
    tril = causal.astype(F32)
    triu = (row_i <= col_i).astype(F32)

    qs, ks, vs = [], [], []
    for h in range(HEADS):
        hs = slice(h * HEAD_DIM, (h + 1) * HEAD_DIM)
        qs.append(jnp.dot(xcb[:, hs], wq_ref[h], preferred_element_type=F32))
        ks.append(jnp.dot(xcb[:, hs], wk_ref[h], preferred_element_type=F32) * (HEAD_DIM ** -0.5))
        vs.append(jnp.dot(xmb[:, hs], wv_ref[h], preferred_element_type=F32))

    for j in range(tt // CHUNK):
        rs = slice(j * CHUNK, (j + 1) * CHUNK)
        gcol = gt_ref[0, rs, :] + gb_ref[...]
        lf_col = jax.nn.log_sigmoid(gcol)
        bcum_col = jnp.dot(tril, lf_col, preferred_element_type=F32, precision=lax.Precision.HIGHEST)
        grow = gcol.T
        lf_row = jax.nn.log_sigmoid(grow[0:SUBLANES, :])
        bcum_row = jnp.dot(lf_row, triu, preferred_element_type=F32, precision=lax.Precision.HIGHEST)
        for h in range(HEADS):
            hs = slice(h * HEAD_DIM, (h + 1) * HEAD_DIM)
            qc, kc, vc = qs[h][rs], ks[h][rs], vs[h][rs]
            bc = bcum_col[:, HEADS + h:HEADS + h + 1]
            br = bcum_row[HEADS + h:HEADS + h + 1, :]
            ic = gcol[:, h:h + 1]
            ir = grow[h:h + 1, :]
            m_prev = m_ref[h:h + 1, 0:1]
            log_w = jnp.where(causal, bc - br + ir, -jnp.inf)
            log_inter = bc + m_prev
            m_t = jnp.maximum(log_inter, jnp.max(log_w, axis=-1, keepdims=True))
            w = jnp.exp(log_w - m_t)
            inter = jnp.exp(log_inter - m_t)
            qcb, kcb, vcb = qc.astype(BF16), kc.astype(BF16), vc.astype(BF16)
            s = lax.dot_general(qcb, kcb, (((1,), (1,)), ((), ())), preferred_element_type=F32)
            sw = s * w
            c_prev = c_ref[h]
            n_prev = n_ref[h:h + 1, :]
            qcmem = lax.dot_general(qcb, c_prev.astype(BF16), (((1,), (1,)), ((), ())),
                                    preferred_element_type=F32)
            num = jnp.dot(sw.astype(BF16), vcb, preferred_element_type=F32) + inter * qcmem
            den = jnp.sum(sw, axis=-1, keepdims=True) + inter * jnp.sum(qc * n_prev, axis=-1, keepdims=True)
            hh = num / jnp.maximum(jnp.abs(den), jnp.exp(-m_t))
            h_ref[rs, hs] = hh
            b_last = bc[CHUNK - 1:CHUNK, :]
            m_new = m_t[CHUNK - 1:CHUNK, :]
            w_end = jnp.exp(b_last - bc + ic - m_new)
            decay = jnp.exp(b_last + m_prev - m_new)
            vw = (vc * w_end).astype(BF16)
            c_ref[h] = decay * c_prev + lax.dot_general(vw, kcb, (((0,), (0,)), ((), ())),
                                                        preferred_element_type=F32)
            n_ref[h:h + 1, :] = decay * n_prev + jnp.sum(w_end * kc, axis=0, keepdims=True)
            m_ref[h:h + 1, :] = jnp.broadcast_to(m_new, (1, LANES))

    z = z_ref[0]
    outs = []
    for h in range(HEADS):
        hs = slice(h * HEAD_DIM, (h + 1) * HEAD_DIM)
        hg = jax.nn.sigmoid(z[:, hs]) * h_ref[:, hs]
        mu = jnp.mean(hg, axis=-1, keepdims=True)
        dv = hg - mu
        var = jnp.mean(dv * dv, axis=-1, keepdims=True)
        outs.append(dv * lax.rsqrt(var + EPS))
    hn = jnp.concatenate(outs, axis=-1)
    hn = hn * gain_ref[...] + skip_ref[...] * xc
    o_ref[0] = jnp.dot(hn.astype(BF16), wp_ref[...], preferred_element_type=F32)


def _mlstm(proj3, conv_w, conv_b, w_q, w_k, w_v, gate_bias, gain, skip, w_proj):
    nb, seq, _ = proj3.shape
    tt = min(MLSTM_TT, seq)
    kern = functools.partial(_mlstm_kernel, tt=tt)
    return pl.pallas_call(
        kern,
        grid=(nb, seq // tt),
        in_specs=[
            pl.BlockSpec((1, tt, MLSTM_WIDTH), lambda b, t: (b, t, COL_XM // MLSTM_WIDTH)),
            pl.BlockSpec((1, tt, MLSTM_WIDTH), lambda b, t: (b, t, COL_Z // MLSTM_WIDTH)),
            pl.BlockSpec((1, tt, LANES), lambda b, t: (b, t, COL_GATE // LANES)),
            _const_spec((CONV_WIDTH, MLSTM_WIDTH)),
            _const_spec((1, MLSTM_WIDTH)),
            _const_spec((HEADS, HEAD_DIM, HEAD_DIM)),
            _const_spec((HEADS, HEAD_DIM, HEAD_DIM)),
            _const_spec((HEADS, HEAD_DIM, HEAD_DIM)),
            _const_spec((1, LANES)),
            _const_spec((1, MLSTM_WIDTH)),
            _const_spec((1, MLSTM_WIDTH)),
            _const_spec((MLSTM_WIDTH, D_MODEL)),
        ],
        out_specs=pl.BlockSpec((1, tt, D_MODEL), lambda b, t: (b, t, 0)),
        out_shape=jax.ShapeDtypeStruct((nb, seq, D_MODEL), F32),
        scratch_shapes=[
            pltpu.VMEM((tt + SUBLANES, MLSTM_WIDTH), F32),
            pltpu.VMEM((HEADS, HEAD_DIM, HEAD_DIM), F32),
            pltpu.VMEM((SUBLANES, HEAD_DIM), F32),
            pltpu.VMEM((SUBLANES, LANES), F32),
            pltpu.VMEM((tt, MLSTM_WIDTH), F32),
        ],
        compiler_params=_cparams(("arbitrary", "arbitrary")),
        name="mlstm",
    )(proj3, proj3, proj3, conv_w, conv_b, w_q, w_k, w_v, gate_bias, gain, skip, w_proj)


def _merge_kernel(ga_ref, gb_ref, ya_ref, yb_ref, x_ref, wo_ref, g2_ref, wq_ref, h1_ref, hn_ref, q_ref):
    merged = jax.nn.sigmoid(ga_ref[...]) * ya_ref[...] + jax.nn.sigmoid(gb_ref[...]) * yb_ref[...]
    h1 = x_ref[...] + jnp.dot(merged.astype(BF16), wo_ref[...], preferred_element_type=F32)
    h1_ref[...] = h1.reshape(h1_ref.shape)
    hn = h1 * lax.rsqrt(jnp.mean(h1 * h1, axis=-1, keepdims=True) + EPS) * g2_ref[...]
    hn_ref[...] = hn.reshape(hn_ref.shape)
    q_ref[...] = jnp.dot(hn.astype(BF16), wq_ref[...], preferred_element_type=F32).astype(BF16)


def _merge(proj, y_a, y_b, x2, w_out, g2, w_query):
    t = x2.shape[0]
    tm = min(MERGE_TM, t)
    row = lambda c: pl.BlockSpec((tm, D_MODEL), lambda i: (i, c))
    out_sd = jax.ShapeDtypeStruct((t, D_MODEL), BF16)
    dense = pl.BlockSpec((tm, ROW_TILES, LANES), lambda i: (i, 0, 0))
    dense_sd = jax.ShapeDtypeStruct((t, ROW_TILES, LANES), F32)
    return pl.pallas_call(
        _merge_kernel,
        grid=(t // tm,),
        in_specs=[row(COL_GA // D_MODEL), row(COL_GB // D_MODEL), row(0), row(0), row(0),
                  _const_spec((D_MODEL, D_MODEL)), _const_spec((1, D_MODEL)), _const_spec((D_MODEL, D_MODEL))],
        out_specs=[dense, dense, row(0)],
        out_shape=[dense_sd, dense_sd, out_sd],
        compiler_params=_cparams(("arbitrary",)),
        name="merge",
    )(proj, proj, y_a, y_b, x2, w_out, g2, w_query)


_CAND_ROWS = [PEER_TOPK // (i + 1) for i in range(PEER_TOPK)]
_CAND_OFFS = [int(v) for v in np.cumsum([0] + _CAND_ROWS[:-1])]
_NCAND = int(sum(_CAND_ROWS))
_NCAND_PAD = 56


def _extract_topk(s, payload, k, val_ref, pay_ref):
    n = s.shape[0]
    big = jnp.int32(2 ** 30)
    for r in range(k):
        m = jnp.max(s, axis=0, keepdims=True)
        sel = jnp.min(jnp.where(s == m, payload, big), axis=0, keepdims=True)
        val_ref[r:r + 1, :] = m
        pay_ref[r:r + 1, :] = sel
        s = jnp.where(payload == sel, -jnp.inf, s)


def _route_kernel(q_ref, k1_ref, k2_ref, e_ref, g_ref, rec_ref, cnt_ref,
                  v1_ref, i1_ref, v2_ref, i2_ref, cv_ref, ce_ref, cp_ref, tv_ref, tp_ref, ea_ref, gate_ref,
                  rows_ref, lhs_ref, *, tb):
    key_iota = lax.broadcasted_iota(jnp.int32, (PEER_NKEYS, tb), 0)
    cand_iota = lax.broadcasted_iota(jnp.int32, (_NCAND_PAD, tb), 0)
    for h in range(PEER_HEADS):
        q1 = q_ref[:, h * 2 * PEER_HALF:h * 2 * PEER_HALF + PEER_HALF].astype(BF16)
        q2 = q_ref[:, h * 2 * PEER_HALF + PEER_HALF:(h + 1) * 2 * PEER_HALF].astype(BF16)
        s1 = lax.dot_general(k1_ref[h], q1, (((1,), (1,)), ((), ())), preferred_element_type=F32)
        s2 = lax.dot_general(k2_ref[h], q2, (((1,), (1,)), ((), ())), preferred_element_type=F32)
        _extract_topk(s1, key_iota, PEER_TOPK, v1_ref, i1_ref)
        _extract_topk(s2, key_iota, PEER_TOPK, v2_ref, i2_ref)
        cv_ref[...] = jnp.full((_NCAND_PAD, tb), -jnp.inf, F32)
        ce_ref[...] = jnp.zeros((_NCAND_PAD, tb), jnp.int32)
        cp_ref[...] = cand_iota + jnp.int32(1 << 20)
        for i in range(PEER_TOPK):
            n_i, off = _CAND_ROWS[i], _CAND_OFFS[i]
            cv_ref[off:off + n_i, :] = v1_ref[i:i + 1, :] + v2_ref[0:n_i, :]
            ce_ref[off:off + n_i, :] = i1_ref[i:i + 1, :] * PEER_NKEYS + i2_ref[0:n_i, :]
            cp_ref[off:off + n_i, :] = i * PEER_TOPK + lax.broadcasted_iota(jnp.int32, (n_i, tb), 0)
        cv = cv_ref[...]
        cp = cp_ref[...]
        _extract_topk(cv, cp, PEER_TOPK, tv_ref, tp_ref)
        ce = ce_ref[...]
        tv = tv_ref[...]
        ex = jnp.exp(tv - tv[0:1, :])
        gate_ref[h * PEER_TOPK:(h + 1) * PEER_TOPK, :] = ex / jnp.sum(ex, axis=0, keepdims=True)
        for r in range(PEER_TOPK):
            ea_ref[h * PEER_TOPK + r:h * PEER_TOPK + r + 1, :] = jnp.sum(
                jnp.where(cp == tp_ref[r:r + 1, :], ce, 0), axis=0, keepdims=True)

    e_all = ea_ref[...]
    g_all = gate_ref[...]
    chunk = e_all >> PEER_SHIFT
    e_loc = e_all & (PEER_EC - 1)
    ri = lax.broadcasted_iota(jnp.int32, (HK, HK), 0)
    ci = lax.broadcasted_iota(jnp.int32, (HK, HK), 1)
    lower = (ci < ri).astype(BF16)
    dest = jnp.zeros((HK, tb), F32)
    gstart = jnp.zeros((1, tb), F32)
    rows_ref[...] = jnp.zeros(rows_ref.shape, F32)
    for c in range(PEER_NCHUNK):
        mask = chunk == c
        maskf = mask.astype(F32)
        rank = jnp.dot(lower, mask.astype(BF16), preferred_element_type=F32)
        grp = jnp.floor(rank * (1.0 / PEER_G))
        dest = dest + maskf * ((rank - grp * PEER_G) * PEER_TGROUPS + gstart + grp)
        cnt = jnp.sum(maskf, axis=0, keepdims=True).astype(jnp.int32)
        ng = jnp.maximum((cnt + (PEER_G - 1)) >> 3, 1).astype(F32)
        rows_ref[c:c + 1, :] = ng
        rows_ref[SUBLANES + c:SUBLANES + c + 1, :] = gstart
        gstart = gstart + ng
    dest_i = dest.astype(jnp.int32)
    for k in range(PEER_SLOTS):
        if k % PEER_TGROUPS == PEER_TGROUPS - 1:
            e_ref[k:k + 1, :] = jnp.zeros((1, tb), jnp.int32)
            g_ref[k:k + 1, :] = jnp.zeros((1, tb), F32)
            continue
        sel = dest_i == k
        e_ref[k:k + 1, :] = jnp.sum(jnp.where(sel, e_loc, 0), axis=0, keepdims=True)
        g_ref[k:k + 1, :] = jnp.sum(jnp.where(sel, g_all, 0.0), axis=0, keepdims=True)

    ng8 = rows_ref[0:SUBLANES, :]
    ti = lax.broadcasted_iota(jnp.int32, (tb, tb), 0)
    tj = lax.broadcasted_iota(jnp.int32, (tb, tb), 1)
    before = (ti < tj).astype(BF16)
    off8 = jnp.dot(ng8.astype(BF16), before, preferred_element_type=F32)
    rows_ref[2 * SUBLANES:3 * SUBLANES, :] = off8
    n8 = jnp.sum(ng8, axis=1, keepdims=True).astype(jnp.int32)
    npad8 = (-n8) & (PEER_LIST_ALIGN - 1)
    cnt_ref[...] = jnp.broadcast_to(n8 + npad8, (SUBLANES, LANES))
    cols = rows_ref[...].T
    tok_row = lax.broadcasted_iota(jnp.int32, (1, tb), 1).astype(F32)
    smax = tb * (HK // PEER_G)
    rec_ref[...] = jnp.zeros(rec_ref.shape, jnp.int32)
    for c in range(PEER_NCHUNK):
        off_row = off8[c:c + 1, :]
        off_hi = jnp.floor(off_row * (1.0 / 64.0))
        lhs_ref[...] = jnp.zeros(lhs_ref.shape, F32)
        lhs_ref[0:1, :] = tok_row
        lhs_ref[1:2, :] = rows_ref[SUBLANES + c:SUBLANES + c + 1, :]
        lhs_ref[2:3, :] = off_hi
        lhs_ref[3:4, :] = off_row - 64.0 * off_hi
        lhs = lhs_ref[...].astype(BF16)
        ng_col = cols[:, c:c + 1]
        off_col = cols[:, 2 * SUBLANES + c:2 * SUBLANES + c + 1]
        n_c = n8[c:c + 1, :].astype(F32)
        npad_c = npad8[c:c + 1, :].astype(F32)
        list_len = (n8 + npad8)[c, 0]

        def list_tile(j):
            s = (lax.broadcasted_iota(jnp.int32, (1, ROUTE_ST), 1) + j * ROUTE_ST).astype(F32) - npad_c
            onehot = jnp.logical_and(off_col <= s, s < off_col + ng_col).astype(BF16)
            r = jnp.dot(lhs, onehot, preferred_element_type=F32)
            tok = r[0:1, :]
            real = jnp.logical_and(s >= 0.0, s < n_c)
            grp = jnp.where(real, r[1:2, :] + s - (r[2:3, :] * 64.0 + r[3:4, :]), PEER_TGROUPS - 1.0)
            gslot = (grp * tb + tok).astype(jnp.int32) + (pl.program_id(0) & 1) * (tb * PEER_TGROUPS)
            tile = slice(j * ROUTE_ST, (j + 1) * ROUTE_ST)
            rec_ref[c:c + 1, tile] = tok.astype(jnp.int32)
            rec_ref[PEER_NCHUNK + c:PEER_NCHUNK + c + 1, tile] = gslot

        for j in range(smax // ROUTE_ST):
            pl.when(list_len > j * ROUTE_ST)(functools.partial(list_tile, j))


def _route(qr, key1, key2):
    t = qr.shape[0]
    tb = min(PEER_TB, t)
    nblk = t // tb
    smax = tb * (HK // PEER_G)
    kern = functools.partial(_route_kernel, tb=tb)
    col = lambda r: pl.BlockSpec((r, tb), lambda i: (0, i))
    f_s = lambda r: pltpu.VMEM((r, tb), F32)
    i_s = lambda r: pltpu.VMEM((r, tb), jnp.int32)
    return pl.pallas_call(
        kern,
        grid=(nblk,),
        in_specs=[pl.BlockSpec((tb, D_MODEL), lambda i: (i, 0)),
                  _const_spec((PEER_HEADS, PEER_NKEYS, PEER_HALF)),
                  _const_spec((PEER_HEADS, PEER_NKEYS, PEER_HALF))],
        out_specs=[col(PEER_SLOTS), col(PEER_SLOTS),
                   pl.BlockSpec((SUBLANES, smax), lambda i: (i, 0)),
                   pl.BlockSpec((SUBLANES, LANES), lambda i: (i, 0))],
        out_shape=[jax.ShapeDtypeStruct((PEER_SLOTS, t), jnp.int32), jax.ShapeDtypeStruct((PEER_SLOTS, t), F32),
                   jax.ShapeDtypeStruct((nblk * SUBLANES, smax), jnp.int32),
                   jax.ShapeDtypeStruct((nblk * SUBLANES, LANES), jnp.int32)],
        scratch_shapes=[f_s(PEER_TOPK), i_s(PEER_TOPK), f_s(PEER_TOPK), i_s(PEER_TOPK),
                        f_s(_NCAND_PAD), i_s(_NCAND_PAD), i_s(_NCAND_PAD),
                        f_s(PEER_TOPK), i_s(PEER_TOPK), i_s(HK), f_s(HK),
                        f_s(LANES), f_s(SUBLANES)],
        compiler_params=_cparams(("arbitrary",)),
        name="route",
    )(qr, key1, key2)


def _peer_kernel(c_ref, cnt_ref, e_hbm, g_hbm, rec_hbm, t_ref, u_ref, v_ref, acc_ref, o_ref,
                 tok_sm, slot_sm, sems, a0_ref, a1_ref, *lists_and_partials, tb):
    c = c_ref[0]
    b = pl.program_id(0)
    ngs = tb * PEER_TGROUPS
    e_sms = lists_and_partials[0:PEER_G]
    g_sms = lists_and_partials[PEER_G:2 * PEER_G]
    p_refs = lists_and_partials[2 * PEER_G:]
    smax = rec_hbm.shape[1]
    par = b & 1

    def list_copies(blk, half):
        rec_dst = pl.ds(half * smax, smax)
        cps = [pltpu.make_async_copy(rec_hbm.at[blk * SUBLANES + c], tok_sm.at[rec_dst], sems.at[half, 0]),
               pltpu.make_async_copy(rec_hbm.at[blk * SUBLANES + PEER_NCHUNK + c], slot_sm.at[rec_dst],
                                     sems.at[half, 1])]
        for k in range(PEER_G):
            dst = pl.ds(half * ngs, ngs)
            cps.append(pltpu.make_async_copy(e_hbm.at[k, pl.ds(blk * ngs, ngs)], e_sms[k].at[dst],
                                             sems.at[half, 2 + k]))
            cps.append(pltpu.make_async_copy(g_hbm.at[k, pl.ds(blk * ngs, ngs)], g_sms[k].at[dst],
                                             sems.at[half, 2 + PEER_G + k]))
        return cps

    @pl.when(b == 0)
    def _():
        for cp in list_copies(b, par):
            cp.start()

    @pl.when(b + 1 < pl.num_programs(0))
    def _():
        for cp in list_copies(b + 1, 1 - par):
            cp.start()

    for cp in list_copies(b, par):
        cp.wait()
    n = cnt_ref[b * PEER_NCHUNK + c]
    rec0 = par * smax

    def tree_sum(xs):
        while len(xs) > 1:
            xs = [xs[i] + xs[i + 1] for i in range(0, len(xs) - 1, 2)] + ([xs[-1]] if len(xs) % 2 else [])
        return xs[0]

    w = PEER_UNROLL
    nb = n // w
    last = jnp.maximum(nb - 1, 0)

    def decode(batch):
        s0 = rec0 + batch * w
        return tuple(tok_sm[s0 + j] for j in range(w)), tuple(slot_sm[s0 + j] for j in range(w))

    def scores(groups):
        toks, gss = groups
        width = len(toks)
        for j in range(width):
            t_lo = t_ref[toks[j], 0:SUBLANES, :]
            t_hi = t_ref[toks[j], SUBLANES:, :]
            for k in range(PEER_G):
                u = u_ref[e_sms[k][gss[j]]].astype(F32)
                p_refs[j][k * SUBLANES:(k + 1) * SUBLANES, :] = t_lo * u[0:SUBLANES] + t_hi * u[SUBLANES:]
        rs = [tree_sum([p_refs[j][pl.ds(i, PEER_G, stride=SUBLANES), :] for i in range(SUBLANES)])
              for j in range(width)]
        return jnp.concatenate(rs, axis=0)

    def activate(partials):
        score = jnp.sum(partials, axis=-1, keepdims=True)
        return jnp.broadcast_to(jax.nn.gelu(score), partials.shape)

    def store_act(a_ref, act):
        a_ref[...] = act

    def update(a_ref, groups, carry):
        prev_tok, acc_lo, acc_hi = carry
        toks, gss = groups
        for j in range(len(toks)):
            lo, hi = [], []
            for k in range(PEER_G):
                pair = j * PEER_G + k
                a = a_ref[pair:pair + 1, :] * g_sms[k][gss[j]]
                v = v_ref[e_sms[k][gss[j]]].astype(F32)
                lo.append(a * v[0:SUBLANES])
                hi.append(a * v[SUBLANES:])
            new = toks[j] != prev_tok
            acc_lo = jnp.where(new, acc_ref[toks[j], 0:SUBLANES, :], acc_lo) + tree_sum(lo)
            acc_hi = jnp.where(new, acc_ref[toks[j], SUBLANES:, :], acc_hi) + tree_sum(hi)
            o_ref[toks[j], 0:SUBLANES, :] = acc_lo
            o_ref[toks[j], SUBLANES:, :] = acc_hi
            prev_tok = toks[j]
        return prev_tok, acc_lo, acc_hi

    zero = jnp.zeros((SUBLANES, LANES), F32)
    store_act(a0_ref, activate(scores(decode(0))))
    partials = scores(decode(jnp.minimum(1, last)))

    def step(i, a_store, a_load, partials, cr):
        act = activate(partials)
        partials = scores(decode(jnp.minimum(i, last)))
        cr = update(a_load, decode(i - 2), cr)
        store_act(a_store, act)
        return partials, cr

    def two_steps(h, state):
        partials, cr = state
        i = 2 * h + 2
        partials, cr = step(i, a1_ref, a0_ref, partials, cr)
        partials, cr = step(i + 1, a0_ref, a1_ref, partials, cr)
        return partials, cr

    lax.fori_loop(0, nb // 2, two_steps, (partials, (jnp.int32(-1), zero, zero)))


def _peer(e_flat, g_flat, rec, cnt, t3, u3, v3, acc3):
    t = t3.shape[0]
    tb = min(PEER_TB, t)
    kern = functools.partial(_peer_kernel, tb=tb)
    tok_spec = pl.BlockSpec((tb, ROW_TILES, LANES), lambda b, c, n: (b, 0, 0))
    tab_spec = pl.BlockSpec((PEER_EC, ROW_TILES, LANES), lambda b, c, n: (c[0], 0, 0),
                            pipeline_mode=pl.Buffered(1))
    call = pl.pallas_call(
        kern,
        grid_spec=pltpu.PrefetchScalarGridSpec(
            num_scalar_prefetch=2,
            grid=(t // tb,),
            in_specs=[pl.BlockSpec(memory_space=pl.ANY), pl.BlockSpec(memory_space=pl.ANY),
                      pl.BlockSpec(memory_space=pl.ANY), tok_spec, tab_spec, tab_spec, tok_spec],
            out_specs=tok_spec,
            scratch_shapes=[
                pltpu.SMEM((2 * rec.shape[1],), jnp.int32),
                pltpu.SMEM((2 * rec.shape[1],), jnp.int32),
                pltpu.SemaphoreType.DMA((2, 2 + 2 * PEER_G)),
                pltpu.VMEM((PEER_UNROLL * PEER_G, LANES), F32),
                pltpu.VMEM((PEER_UNROLL * PEER_G, LANES), F32),
            ] + [pltpu.SMEM((2 * tb * PEER_TGROUPS,), jnp.int32)] * PEER_G
              + [pltpu.SMEM((2 * tb * PEER_TGROUPS,), F32)] * PEER_G
              + [pltpu.VMEM((PEER_G * SUBLANES, LANES), F32)] * PEER_UNROLL,
        ),
        out_shape=jax.ShapeDtypeStruct(acc3.shape, F32),
        input_output_aliases={8: 0},
        compiler_params=_cparams(("arbitrary",)),
        name="peer",
    )
    acc = acc3
    for c in range(PEER_NCHUNK):
        acc = call(jnp.full((1,), c, jnp.int32), cnt, e_flat, g_flat, rec, t3, u3, v3, acc)
    return acc


def _final_kernel(a_ref, g_ref, o_ref):
    a = a_ref[...].reshape(o_ref.shape)
    ms = jnp.mean(a * a, axis=-1, keepdims=True)
    o_ref[...] = a * lax.rsqrt(ms + EPS) * g_ref[...]


def _final(acc2, g):
    t = acc2.shape[0]
    tm = min(FINAL_TM, t)
    return pl.pallas_call(
        _final_kernel,
        grid=(t // tm,),
        in_specs=[pl.BlockSpec((tm, ROW_TILES, LANES), lambda i: (i, 0, 0)),
                  pl.BlockSpec((1, D_MODEL), lambda i: (0, 0))],
        out_specs=pl.BlockSpec((tm, D_MODEL), lambda i: (i, 0)),
        out_shape=jax.ShapeDtypeStruct((t, D_MODEL), F32),
        compiler_params=_cparams(("arbitrary",)),
        name="final",
    )(acc2, g)


def _layer(x, norm1_g, w_in, lam_re, lam_im, log_dt, b_re, b_im, c_re, c_im, d_skip, w_glu, conv_w, conv_b,
           w_q, w_k, w_v, b_i, b_f, mh_gain, mlstm_skip, w_mlstm_out, w_out, norm2_g, w_query, key1, key2,
           expert_u, expert_v):
    nb, seq, d = x.shape
    t = nb * seq
    x2 = x.reshape(t, d)
    row = lambda a: a.reshape(1, -1).astype(F32)

    o_ssm, o_xm, o_z, o_i, o_f, o_ga, o_gb = 0, 512, 2048, 3584, 3588, 3592, 5640
    zeros = lambda n: jnp.zeros((d, n), w_in.dtype)
    w_cat = jnp.concatenate([
        w_in[:, o_xm:o_z], w_in[:, o_z:o_i], w_in[:, o_ssm:o_xm], w_in[:, o_i:o_ga],
        zeros(COL_GA - COL_GATE - 2 * HEADS), w_in[:, o_ga:o_gb], w_in[:, o_gb:]], axis=1).astype(BF16)
    proj = _inproj(x2, row(norm1_g), w_cat)
    proj3 = proj.reshape(nb, seq, PROJ_W)

    bblk, a_re, a_im, cblk = _s5_params(lam_re, lam_im, log_dt, b_re, b_im, c_re, c_im)
    y_a = _s5(proj3, bblk, a_re, a_im, cblk, row(d_skip), w_glu.astype(BF16))

    gate_bias = jnp.concatenate([b_i.astype(F32), b_f.astype(F32), jnp.zeros((LANES - 2 * HEADS,), F32)]).reshape(1, LANES)
    y_b = _mlstm(proj3, conv_w.astype(F32), row(conv_b), w_q.astype(BF16), w_k.astype(BF16), w_v.astype(BF16),
                 gate_bias, row(mh_gain), row(mlstm_skip), w_mlstm_out.astype(BF16))

    h1, hn2, qr = _merge(proj, y_a.reshape(t, d), y_b.reshape(t, d), x2, w_out.astype(BF16), row(norm2_g),
                         w_query.astype(BF16))

    e_t, g_t, rec, cnt8 = _route(qr, key1.astype(BF16), key2.astype(BF16))
    tb = min(PEER_TB, t)
    by_pair = lambda a: a.reshape(PEER_G, PEER_TGROUPS, t // tb, tb).transpose(0, 2, 1, 3).reshape(PEER_G, -1)
    e_flat = by_pair(e_t)
    g_flat = by_pair(g_t)
    cnt = cnt8[:, 0].reshape(-1, SUBLANES)[:, :PEER_NCHUNK].reshape(-1)

    dense = lambda a: a.reshape(a.shape[0], ROW_TILES, LANES)
    return _peer(e_flat, g_flat, rec, cnt, hn2, dense(expert_u.astype(BF16)), dense(expert_v.astype(BF16)), h1)


def kernel(x, norm1_g, w_in, lam_re, lam_im, log_dt, b_re, b_im, c_re, c_im, d_skip, w_glu, conv_w, conv_b, w_q, w_k, w_v, b_i, b_f, mh_gain, mlstm_skip, w_mlstm_out, w_out, norm2_g, w_query, key1, key2, expert_u, expert_v, norm_f_g):
    depth = w_in.shape[0]
    nb, seq, d = x.shape
    h = x
    for l in range(depth):
        acc = _layer(h, norm1_g[l], w_in[l], lam_re[l], lam_im[l], log_dt[l], b_re[l], b_im[l], c_re[l], c_im[l],
                     d_skip[l], w_glu[l], conv_w[l], conv_b[l], w_q[l], w_k[l], w_v[l], b_i[l], b_f[l], mh_gain[l],
                     mlstm_skip[l], w_mlstm_out[l], w_out[l], norm2_g[l], w_query[l], key1[l], key2[l],
                     expert_u[l], expert_v[l])
        h = acc.reshape(nb, seq, d) if l + 1 < depth else None
    out = _final(acc, norm_f_g.reshape(1, d).astype(F32))
    return out.reshape(nb, seq, d)
```

```python
import functools
import math

import jax
import jax.numpy as jnp
import numpy as np
from jax import lax
from jax.experimental import pallas as pl
from jax.experimental.pallas import tpu as pltpu

F32 = jnp.float32
BF16 = jnp.bfloat16
EPS = 1e-6

D_MODEL = 2048
SSM_WIDTH = 512
SSM_GROUP = 16
SSM_GROUPS = 32
SSM_STATE = 64
N_STATE = SSM_GROUPS * SSM_STATE
MLSTM_WIDTH = 1536
HEADS = 4
HEAD_DIM = 384
CONV_WIDTH = 4
CHUNK = 64
PEER_HEADS = 8
PEER_HALF = 128
PEER_NKEYS = 128
PEER_TOPK = 16
PEER_EXPERTS = PEER_NKEYS * PEER_NKEYS
HK = PEER_HEADS * PEER_TOPK

LANES = 128
SUBLANES = 8
ROW_TILES = D_MODEL // LANES

COL_XM = 0
COL_Z = 1536
COL_SSM = 3072
COL_GATE = 3584
COL_GA = 4096
COL_GB = 6144
PROJ_W = 8192

INPROJ_TM = 1024
INPROJ_TN = 1024
S5_LT = 32
MLSTM_TT = 256
MERGE_TM = 256
ROUTE_ST = 512
PEER_EC = 4096
PEER_NCHUNK = PEER_EXPERTS // PEER_EC
PEER_SHIFT = int(math.log2(PEER_EC))
PEER_TB = 128
PEER_G = 8
PEER_TGROUPS = 20
PEER_SLOTS = PEER_TGROUPS * PEER_G
PEER_UNROLL = 4
PEER_LIST_ALIGN = 2 * PEER_UNROLL
FINAL_TM = 512
assert PEER_G == SUBLANES and (HK + PEER_NCHUNK * (PEER_G - 1)) // PEER_G <= PEER_TGROUPS - 1
assert 2 * PEER_NCHUNK <= SUBLANES

VMEM_LIMIT = 56 * 1024 * 1024


def _cparams(sem):
    return pltpu.CompilerParams(dimension_semantics=sem, vmem_limit_bytes=VMEM_LIMIT)


def _const_spec(shape):
    nd = len(shape)
    return pl.BlockSpec(shape, lambda *_: (0,) * nd, pipeline_mode=pl.Buffered(1))


def _inproj_kernel(x_ref, g_ref, w_ref, o_ref, hn_ref):
    @pl.when(pl.program_id(1) == 0)
    def _():
        x = x_ref[...]
        y = x * lax.rsqrt(jnp.mean(x * x, axis=-1, keepdims=True) + EPS) * g_ref[...]
        hn_ref[...] = y.astype(BF16)

    o_ref[...] = jnp.dot(hn_ref[...], w_ref[...], preferred_element_type=F32)


def _inproj(x2, g, w_cat):
    t = x2.shape[0]
    tm = min(INPROJ_TM, t)
    return pl.pallas_call(
        _inproj_kernel,
        grid=(t // tm, PROJ_W // INPROJ_TN),
        in_specs=[
            pl.BlockSpec((tm, D_MODEL), lambda i, n: (i, 0)),
            pl.BlockSpec((1, D_MODEL), lambda i, n: (0, 0)),
            pl.BlockSpec((D_MODEL, INPROJ_TN), lambda i, n: (0, n)),
        ],
        out_specs=pl.BlockSpec((tm, INPROJ_TN), lambda i, n: (i, n)),
        out_shape=jax.ShapeDtypeStruct((t, PROJ_W), F32),
        scratch_shapes=[pltpu.VMEM((tm, D_MODEL), BF16)],
        compiler_params=_cparams(("arbitrary", "arbitrary")),
        name="inproj",
    )(x2, g, w_cat)


def _s5_kernel(u_ref, perm_ref, permt_ref, bblk_ref, are_ref, aim_ref, cblk_ref, d_ref, wglu_ref, o_ref,
               x_ref, st_ref, *, nb, lt):
    @pl.when(pl.program_id(0) == 0)
    def _():
        st_ref[...] = jnp.zeros_like(st_ref)

    rows = nb * lt
    u = u_ref[...].reshape(rows, SSM_WIDTH)
    ut = jnp.dot(perm_ref[...], u.astype(BF16), preferred_element_type=F32).astype(BF16)
    bu = jnp.dot(ut, bblk_ref[...], preferred_element_type=F32)
    n_tiles = N_STATE // LANES
    for j in range(2 * n_tiles):
        x_ref[j] = bu[:, j * LANES:(j + 1) * LANES]

    tiles_per_pass = 4
    for t0 in range(0, n_tiles, tiles_per_pass):
        tiles = range(t0, t0 + tiles_per_pass)
        a_re = [are_ref[:, j * LANES:(j + 1) * LANES] for j in tiles]
        a_im = [aim_ref[:, j * LANES:(j + 1) * LANES] for j in tiles]

        def step(s, carry):
            r = pl.ds(pl.multiple_of(s * nb, nb), nb)
            out = []
            for i, j in enumerate(tiles):
                s_re, s_im = carry[2 * i], carry[2 * i + 1]
                n_re = a_re[i] * s_re - a_im[i] * s_im + x_ref[j, r, :]
                n_im = a_re[i] * s_im + a_im[i] * s_re + x_ref[n_tiles + j, r, :]
                x_ref[j, r, :] = n_re
                x_ref[n_tiles + j, r, :] = n_im
                out += [n_re, n_im]
            return tuple(out)

        init = []
        for j in tiles:
            init += [st_ref[j], st_ref[n_tiles + j]]
        fin = lax.fori_loop(0, lt, step, tuple(init))
        for i, j in enumerate(tiles):
            st_ref[j] = fin[2 * i]
            st_ref[n_tiles + j] = fin[2 * i + 1]

    xs = jnp.concatenate([x_ref[j].astype(BF16) for j in range(2 * n_tiles)], axis=-1)
    z = jnp.dot(xs, cblk_ref[...], preferred_element_type=F32)
    z = jnp.dot(permt_ref[...], z, preferred_element_type=F32, precision=lax.Precision.HIGHEST)
    y = z + d_ref[...] * u
    y = jax.nn.gelu(y)
    vg = jnp.dot(y.astype(BF16), wglu_ref[...], preferred_element_type=F32)
    out = vg[:, :D_MODEL] * jax.nn.sigmoid(vg[:, D_MODEL:])
    o_ref[...] = out.reshape(nb, lt, D_MODEL)


def _s5(proj3, bblk, a_re, a_im, cblk, d_skip, w_glu):
    nb, seq, _ = proj3.shape
    lt = min(S5_LT, seq)
    kern = functools.partial(_s5_kernel, nb=nb, lt=lt)
    src = (jnp.arange(nb)[None, :] * lt + jnp.arange(lt)[:, None]).reshape(-1)
    perm = jax.nn.one_hot(src, nb * lt, dtype=F32)
    return pl.pallas_call(
        kern,
        grid=(seq // lt,),
        in_specs=[
            pl.BlockSpec((nb, lt, SSM_WIDTH), lambda c: (0, c, COL_SSM // SSM_WIDTH)),
            _const_spec((nb * lt, nb * lt)),
            _const_spec((nb * lt, nb * lt)),
            _const_spec((SSM_WIDTH, 2 * N_STATE)),
            _const_spec((1, N_STATE)),
            _const_spec((1, N_STATE)),
            _const_spec((2 * N_STATE, SSM_WIDTH)),
            _const_spec((1, SSM_WIDTH)),
            _const_spec((SSM_WIDTH, 2 * D_MODEL)),
        ],
        out_specs=pl.BlockSpec((nb, lt, D_MODEL), lambda c: (0, c, 0)),
        out_shape=jax.ShapeDtypeStruct((nb, seq, D_MODEL), F32),
        scratch_shapes=[pltpu.VMEM((2 * N_STATE // LANES, nb * lt, LANES), F32),
                        pltpu.VMEM((2 * N_STATE // LANES, nb, LANES), F32)],
        compiler_params=_cparams(("arbitrary",)),
        name="s5",
    )(proj3, perm.astype(BF16), perm.T, bblk, a_re, a_im, cblk, d_skip, w_glu)


def _s5_params(lam_re, lam_im, log_dt, b_re, b_im, c_re, c_im):
    g, p, h = SSM_GROUPS, SSM_STATE, SSM_GROUP
    lam = lax.complex(lam_re.astype(F32), lam_im.astype(F32))
    dt = jnp.exp(log_dt.astype(F32))[:, None]
    a_bar = jnp.exp(lam * dt)
    b_bar = ((a_bar - 1.0) / lam)[..., None] * lax.complex(b_re.astype(F32), b_im.astype(F32))
    eye = jnp.eye(g, dtype=F32)
    bb_re = jnp.einsum("gph,gk->ghkp", jnp.real(b_bar), eye).reshape(g * h, g * p)
    bb_im = jnp.einsum("gph,gk->ghkp", jnp.imag(b_bar), eye).reshape(g * h, g * p)
    bblk = jnp.concatenate([bb_re, bb_im], axis=1).astype(BF16)
    cc_re = jnp.einsum("ghp,gk->kpgh", c_re.astype(F32), eye).reshape(g * p, g * h)
    cc_im = jnp.einsum("ghp,gk->kpgh", c_im.astype(F32), eye).reshape(g * p, g * h)
    cblk = jnp.concatenate([cc_re, -cc_im], axis=0).astype(BF16)
    return bblk, jnp.real(a_bar).reshape(1, g * p), jnp.imag(a_bar).reshape(1, g * p), cblk


def _mlstm_kernel(xm_ref, z_ref, gt_ref, cw_ref, cb_ref, wq_ref, wk_ref, wv_ref, gb_ref, gain_ref, skip_ref,
                  wp_ref, o_ref, cbuf_ref, c_ref, n_ref, m_ref, h_ref, *, tt):
    halo = SUBLANES

    @pl.when(pl.program_id(1) == 0)
    def _():
        cbuf_ref[0:halo, :] = jnp.zeros((halo, MLSTM_WIDTH), F32)
        c_ref[...] = jnp.zeros_like(c_ref)
        n_ref[...] = jnp.zeros_like(n_ref)
        m_ref[...] = jnp.zeros_like(m_ref)

    @pl.when(pl.program_id(1) > 0)
    def _():
        cbuf_ref[0:halo, :] = cbuf_ref[tt:tt + halo, :]

    xm = xm_ref[0]
    cbuf_ref[halo:halo + tt, :] = xm
    conv = cb_ref[...]
    for j in range(CONV_WIDTH):
        off = halo - (CONV_WIDTH - 1) + j
        conv = conv + cbuf_ref[off:off + tt, :] * cw_ref[j:j + 1, :]
    xc = conv * jax.nn.sigmoid(conv)
    xcb = xc.astype(BF16)
    xmb = xm.astype(BF16)

    row_i = lax.broadcasted_iota(jnp.int32, (CHUNK, CHUNK), 0)
    col_i = lax.broadcasted_iota(jnp.int32, (CHUNK, CHUNK), 1)
    causal = col_i <= row_i
    tril = causal.astype(F32)
    triu = (row_i <= col_i).astype(F32)

    qs, ks, vs = [], [], []
    for h in range(HEADS):
        hs = slice(h * HEAD_DIM, (h + 1) * HEAD_DIM)
        qs.append(jnp.dot(xcb[:, hs], wq_ref[h], preferred_element_type=F32))
        ks.append(jnp.dot(xcb[:, hs], wk_ref[h], preferred_element_type=F32) * (HEAD_DIM ** -0.5))
        vs.append(jnp.dot(xmb[:, hs], wv_ref[h], preferred_element_type=F32))

    for j in range(tt // CHUNK):
        rs = slice(j * CHUNK, (j + 1) * CHUNK)
        gcol = gt_ref[0, rs, :] + gb_ref[...]
        lf_col = jax.nn.log_sigmoid(gcol)
        bcum_col = jnp.dot(tril, lf_col, preferred_element_type=F32, precision=lax.Precision.HIGHEST)
        grow = gcol.T
        lf_row = jax.nn.log_sigmoid(grow[0:SUBLANES, :])
        bcum_row = jnp.dot(lf_row, triu, preferred_element_type=F32, precision=lax.Precision.HIGHEST)
        for h in range(HEADS):
            hs = slice(h * HEAD_DIM, (h + 1) * HEAD_DIM)
            qc, kc, vc = qs[h][rs], ks[h][rs], vs[h][rs]
            bc = bcum_col[:, HEADS + h:HEADS + h + 1]
            br = bcum_row[HEADS + h:HEADS + h + 1, :]
            ic = gcol[:, h:h + 1]
            ir = grow[h:h + 1, :]
            m_prev = m_ref[h:h + 1, 0:1]
            log_w = jnp.where(causal, bc - br + ir, -jnp.inf)
            log_inter = bc + m_prev
            m_t = jnp.maximum(log_inter, jnp.max(log_w, axis=-1, keepdims=True))
            w = jnp.exp(log_w - m_t)
            inter = jnp.exp(log_inter - m_t)
            qcb, kcb, vcb = qc.astype(BF16), kc.astype(BF16), vc.astype(BF16)
            s = lax.dot_general(qcb, kcb, (((1,), (1,)), ((), ())), preferred_element_type=F32)
            sw = s * w
            c_prev = c_ref[h]
            n_prev = n_ref[h:h + 1, :]
            qcmem = lax.dot_general(qcb, c_prev.astype(BF16), (((1,), (1,)), ((), ())),
                                    preferred_element_type=F32)
            num = jnp.dot(sw.astype(BF16), vcb, preferred_element_type=F32) + inter * qcmem
            den = jnp.sum(sw, axis=-1, keepdims=True) + inter * jnp.sum(qc * n_prev, axis=-1, keepdims=True)
            hh = num / jnp.maximum(jnp.abs(den), jnp.exp(-m_t))
            h_ref[rs, hs] = hh
            b_last = bc[CHUNK - 1:CHUNK, :]
            m_new = m_t[CHUNK - 1:CHUNK, :]
            w_end = jnp.exp(b_last - bc + ic - m_new)
            decay = jnp.exp(b_last + m_prev - m_new)
            vw = (vc * w_end).astype(BF16)
            c_ref[h] = decay * c_prev + lax.dot_general(vw, kcb, (((0,), (0,)), ((), ())),
                                                        preferred_element_type=F32)
            n_ref[h:h + 1, :] = decay * n_prev + jnp.sum(w_end * kc, axis=0, keepdims=True)
            m_ref[h:h + 1, :] = jnp.broadcast_to(m_new, (1, LANES))

    z = z_ref[0]
    outs = []
    for h in range(HEADS):
        hs = slice(h * HEAD_DIM, (h + 1) * HEAD_DIM)
        hg = jax.nn.sigmoid(z[:, hs]) * h_ref[:, hs]
        mu = jnp.mean(hg, axis=-1, keepdims=True)
        dv = hg - mu
        var = jnp.mean(dv * dv, axis=-1, keepdims=True)
        outs.append(dv * lax.rsqrt(var + EPS))
    hn = jnp.concatenate(outs, axis=-1)
    hn = hn * gain_ref[...] + skip_ref[...] * xc
    o_ref[0] = jnp.dot(hn.astype(BF16), wp_ref[...], preferred_element_type=F32)


def _mlstm(proj3, conv_w, conv_b, w_q, w_k, w_v, gate_bias, gain, skip, w_proj):
    nb, seq, _ = proj3.shape
    tt = min(MLSTM_TT, seq)
    kern = functools.partial(_mlstm_kernel, tt=tt)
    return pl.pallas_call(
        kern,
        grid=(nb, seq // tt),
        in_specs=[
            pl.BlockSpec((1, tt, MLSTM_WIDTH), lambda b, t: (b, t, COL_XM // MLSTM_WIDTH)),
            pl.BlockSpec((1, tt, MLSTM_WIDTH), lambda b, t: (b, t, COL_Z // MLSTM_WIDTH)),
            pl.BlockSpec((1, tt, LANES), lambda b, t: (b, t, COL_GATE // LANES)),
            _const_spec((CONV_WIDTH, MLSTM_WIDTH)),
            _const_spec((1, MLSTM_WIDTH)),
            _const_spec((HEADS, HEAD_DIM, HEAD_DIM)),
            _const_spec((HEADS, HEAD_DIM, HEAD_DIM)),
            _const_spec((HEADS, HEAD_DIM, HEAD_DIM)),
            _const_spec((1, LANES)),
            _const_spec((1, MLSTM_WIDTH)),
            _const_spec((1, MLSTM_WIDTH)),
            _const_spec((MLSTM_WIDTH, D_MODEL)),
        ],
        out_specs=pl.BlockSpec((1, tt, D_MODEL), lambda b, t: (b, t, 0)),
        out_shape=jax.ShapeDtypeStruct((nb, seq, D_MODEL), F32),
        scratch_shapes=[
            pltpu.VMEM((tt + SUBLANES, MLSTM_WIDTH), F32),
            pltpu.VMEM((HEADS, HEAD_DIM, HEAD_DIM), F32),
            pltpu.VMEM((SUBLANES, HEAD_DIM), F32),
            pltpu.VMEM((SUBLANES, LANES), F32),
            pltpu.VMEM((tt, MLSTM_WIDTH), F32),
        ],
        compiler_params=_cparams(("arbitrary", "arbitrary")),
        name="mlstm",
    )(proj3, proj3, proj3, conv_w, conv_b, w_q, w_k, w_v, gate_bias, gain, skip, w_proj)


def _merge_kernel(ga_ref, gb_ref, ya_ref, yb_ref, x_ref, wo_ref, g2_ref, wq_ref, h1_ref, hn_ref, q_ref):
    merged = jax.nn.sigmoid(ga_ref[...]) * ya_ref[...] + jax.nn.sigmoid(gb_ref[...]) * yb_ref[...]
    h1 = x_ref[...] + jnp.dot(merged.astype(BF16), wo_ref[...], preferred_element_type=F32)
    h1_ref[...] = h1.reshape(h1_ref.shape)
    hn = h1 * lax.rsqrt(jnp.mean(h1 * h1, axis=-1, keepdims=True) + EPS) * g2_ref[...]
    hn_ref[...] = hn.reshape(hn_ref.shape)
    q_ref[...] = jnp.dot(hn.astype(BF16), wq_ref[...], preferred_element_type=F32).astype(BF16)


def _merge(proj, y_a, y_b, x2, w_out, g2, w_query):
    t = x2.shape[0]
    tm = min(MERGE_TM, t)
    row = lambda c: pl.BlockSpec((tm, D_MODEL), lambda i: (i, c))
    out_sd = jax.ShapeDtypeStruct((t, D_MODEL), BF16)
    dense = pl.BlockSpec((tm, ROW_TILES, LANES), lambda i: (i, 0, 0))
    dense_sd = jax.ShapeDtypeStruct((t, ROW_TILES, LANES), F32)
    return pl.pallas_call(
        _merge_kernel,
        grid=(t // tm,),
        in_specs=[row(COL_GA // D_MODEL), row(COL_GB // D_MODEL), row(0), row(0), row(0),
                  _const_spec((D_MODEL, D_MODEL)), _const_spec((1, D_MODEL)), _const_spec((D_MODEL, D_MODEL))],
        out_specs=[dense, dense, row(0)],
        out_shape=[dense_sd, dense_sd, out_sd],
        compiler_params=_cparams(("arbitrary",)),
        name="merge",
    )(proj, proj, y_a, y_b, x2, w_out, g2, w_query)


_CAND_ROWS = [PEER_TOPK // (i + 1) for i in range(PEER_TOPK)]
_CAND_OFFS = [int(v) for v in np.cumsum([0] + _CAND_ROWS[:-1])]
_NCAND = int(sum(_CAND_ROWS))
_NCAND_PAD = 56


def _extract_topk(s, payload, k, val_ref, pay_ref):
    n = s.shape[0]
    big = jnp.int32(2 ** 30)
    for r in range(k):
        m = jnp.max(s, axis=0, keepdims=True)
        sel = jnp.min(jnp.where(s == m, payload, big), axis=0, keepdims=True)
        val_ref[r:r + 1, :] = m
        pay_ref[r:r + 1, :] = sel
        s = jnp.where(payload == sel, -jnp.inf, s)


def _route_kernel(q_ref, k1_ref, k2_ref, e_ref, g_ref, rec_ref, cnt_ref,
                  v1_ref, i1_ref, v2_ref, i2_ref, cv_ref, ce_ref, cp_ref, tv_ref, tp_ref, ea_ref, gate_ref,
                  rows_ref, lhs_ref, *, tb):
    key_iota = lax.broadcasted_iota(jnp.int32, (PEER_NKEYS, tb), 0)
    cand_iota = lax.broadcasted_iota(jnp.int32, (_NCAND_PAD, tb), 0)
    for h in range(PEER_HEADS):
        q1 = q_ref[:, h * 2 * PEER_HALF:h * 2 * PEER_HALF + PEER_HALF].astype(BF16)
        q2 = q_ref[:, h * 2 * PEER_HALF + PEER_HALF:(h + 1) * 2 * PEER_HALF].astype(BF16)
        s1 = lax.dot_general(k1_ref[h], q1, (((1,), (1,)), ((), ())), preferred_element_type=F32)
        s2 = lax.dot_general(k2_ref[h], q2, (((1,), (1,)), ((), ())), preferred_element_type=F32)
        _extract_topk(s1, key_iota, PEER_TOPK, v1_ref, i1_ref)
        _extract_topk(s2, key_iota, PEER_TOPK, v2_ref, i2_ref)
        cv_ref[...] = jnp.full((_NCAND_PAD, tb), -jnp.inf, F32)
        ce_ref[...] = jnp.zeros((_NCAND_PAD, tb), jnp.int32)
        cp_ref[...] = cand_iota + jnp.int32(1 << 20)
        for i in range(PEER_TOPK):
            n_i, off = _CAND_ROWS[i], _CAND_OFFS[i]
            cv_ref[off:off + n_i, :] = v1_ref[i:i + 1, :] + v2_ref[0:n_i, :]
            ce_ref[off:off + n_i, :] = i1_ref[i:i + 1, :] * PEER_NKEYS + i2_ref[0:n_i, :]
            cp_ref[off:off + n_i, :] = i * PEER_TOPK + lax.broadcasted_iota(jnp.int32, (n_i, tb), 0)
        cv = cv_ref[...]
        cp = cp_ref[...]
        _extract_topk(cv, cp, PEER_TOPK, tv_ref, tp_ref)
        ce = ce_ref[...]
        tv = tv_ref[...]
        ex = jnp.exp(tv - tv[0:1, :])
        gate_ref[h * PEER_TOPK:(h + 1) * PEER_TOPK, :] = ex / jnp.sum(ex, axis=0, keepdims=True)
        for r in range(PEER_TOPK):
            ea_ref[h * PEER_TOPK + r:h * PEER_TOPK + r + 1, :] = jnp.sum(
                jnp.where(cp == tp_ref[r:r + 1, :], ce, 0), axis=0, keepdims=True)

    e_all = ea_ref[...]
    g_all = gate_ref[...]
    chunk = e_all >> PEER_SHIFT
    e_loc = e_all & (PEER_EC - 1)
    ri = lax.broadcasted_iota(jnp.int32, (HK, HK), 0)
    ci = lax.broadcasted_iota(jnp.int32, (HK, HK), 1)
    lower = (ci < ri).astype(BF16)
    dest = jnp.zeros((HK, tb), F32)
    gstart = jnp.zeros((1, tb), F32)
    rows_ref[...] = jnp.zeros(rows_ref.shape, F32)
    for c in range(PEER_NCHUNK):
        mask = chunk == c
        maskf = mask.astype(F32)
        rank = jnp.dot(lower, mask.astype(BF16), preferred_element_type=F32)
        grp = jnp.floor(rank * (1.0 / PEER_G))
        dest = dest + maskf * ((rank - grp * PEER_G) * PEER_TGROUPS + gstart + grp)
        cnt = jnp.sum(maskf, axis=0, keepdims=True).astype(jnp.int32)
        ng = jnp.maximum((cnt + (PEER_G - 1)) >> 3, 1).astype(F32)
        rows_ref[c:c + 1, :] = ng
        rows_ref[SUBLANES + c:SUBLANES + c + 1, :] = gstart
        gstart = gstart + ng
    dest_i = dest.astype(jnp.int32)
    for k in range(PEER_SLOTS):
        if k % PEER_TGROUPS == PEER_TGROUPS - 1:
            e_ref[k:k + 1, :] = jnp.zeros((1, tb), jnp.int32)
            g_ref[k:k + 1, :] = jnp.zeros((1, tb), F32)
            continue
        sel = dest_i == k
        e_ref[k:k + 1, :] = jnp.sum(jnp.where(sel, e_loc, 0), axis=0, keepdims=True)
        g_ref[k:k + 1, :] = jnp.sum(jnp.where(sel, g_all, 0.0), axis=0, keepdims=True)

    ng8 = rows_ref[0:SUBLANES, :]
    ti = lax.broadcasted_iota(jnp.int32, (tb, tb), 0)
    tj = lax.broadcasted_iota(jnp.int32, (tb, tb), 1)
    before = (ti < tj).astype(BF16)
    off8 = jnp.dot(ng8.astype(BF16), before, preferred_element_type=F32)
    rows_ref[2 * SUBLANES:3 * SUBLANES, :] = off8
    n8 = jnp.sum(ng8, axis=1, keepdims=True).astype(jnp.int32)
    npad8 = (-n8) & (PEER_LIST_ALIGN - 1)
    cnt_ref[...] = jnp.broadcast_to(n8 + npad8, (SUBLANES, LANES))
    cols = rows_ref[...].T
    tok_row = lax.broadcasted_iota(jnp.int32, (1, tb), 1).astype(F32)
    smax = tb * (HK // PEER_G)
    rec_ref[...] = jnp.zeros(rec_ref.shape, jnp.int32)
    for c in range(PEER_NCHUNK):
        off_row = off8[c:c + 1, :]
        off_hi = jnp.floor(off_row * (1.0 / 64.0))
        lhs_ref[...] = jnp.zeros(lhs_ref.shape, F32)
        lhs_ref[0:1, :] = tok_row
        lhs_ref[1:2, :] = rows_ref[SUBLANES + c:SUBLANES + c + 1, :]
        lhs_ref[2:3, :] = off_hi
        lhs_ref[3:4, :] = off_row - 64.0 * off_hi
        lhs = lhs_ref[...].astype(BF16)
        ng_col = cols[:, c:c + 1]
        off_col = cols[:, 2 * SUBLANES + c:2 * SUBLANES + c + 1]
        n_c = n8[c:c + 1, :].astype(F32)
        npad_c = npad8[c:c + 1, :].astype(F32)
        for j in range(smax // ROUTE_ST):
            s = (lax.broadcasted_iota(jnp.int32, (1, ROUTE_ST), 1) + j * ROUTE_ST).astype(F32) - npad_c
            onehot = jnp.logical_and(off_col <= s, s < off_col + ng_col).astype(BF16)
            r = jnp.dot(lhs, onehot, preferred_element_type=F32)
            tok = r[0:1, :]
            real = jnp.logical_and(s >= 0.0, s < n_c)
            grp = jnp.where(real, r[1:2, :] + s - (r[2:3, :] * 64.0 + r[3:4, :]), PEER_TGROUPS - 1.0)
            gslot = (grp * tb + tok).astype(jnp.int32) + (pl.program_id(0) & 1) * (tb * PEER_TGROUPS)
            tile = slice(j * ROUTE_ST, (j + 1) * ROUTE_ST)
            rec_ref[c:c + 1, tile] = tok.astype(jnp.int32)
            rec_ref[PEER_NCHUNK + c:PEER_NCHUNK + c + 1, tile] = gslot


def _route(qr, key1, key2):
    t = qr.shape[0]
    tb = min(PEER_TB, t)
    nblk = t // tb
    smax = tb * (HK // PEER_G)
    kern = functools.partial(_route_kernel, tb=tb)
    col = lambda r: pl.BlockSpec((r, tb), lambda i: (0, i))
    f_s = lambda r: pltpu.VMEM((r, tb), F32)
    i_s = lambda r: pltpu.VMEM((r, tb), jnp.int32)
    return pl.pallas_call(
        kern,
        grid=(nblk,),
        in_specs=[pl.BlockSpec((tb, D_MODEL), lambda i: (i, 0)),
                  _const_spec((PEER_HEADS, PEER_NKEYS, PEER_HALF)),
                  _const_spec((PEER_HEADS, PEER_NKEYS, PEER_HALF))],
        out_specs=[col(PEER_SLOTS), col(PEER_SLOTS),
                   pl.BlockSpec((SUBLANES, smax), lambda i: (i, 0)),
                   pl.BlockSpec((SUBLANES, LANES), lambda i: (i, 0))],
        out_shape=[jax.ShapeDtypeStruct((PEER_SLOTS, t), jnp.int32), jax.ShapeDtypeStruct((PEER_SLOTS, t), F32),
                   jax.ShapeDtypeStruct((nblk * SUBLANES, smax), jnp.int32),
                   jax.ShapeDtypeStruct((nblk * SUBLANES, LANES), jnp.int32)],
        scratch_shapes=[f_s(PEER_TOPK), i_s(PEER_TOPK), f_s(PEER_TOPK), i_s(PEER_TOPK),
                        f_s(_NCAND_PAD), i_s(_NCAND_PAD), i_s(_NCAND_PAD),
                        f_s(PEER_TOPK), i_s(PEER_TOPK), i_s(HK), f_s(HK),
                        f_s(LANES), f_s(SUBLANES)],
        compiler_params=_cparams(("arbitrary",)),
        name="route",
    )(qr, key1, key2)


def _peer_kernel(c_ref, cnt_ref, e_hbm, g_hbm, rec_hbm, t_ref, u_ref, v_ref, acc_ref, o_ref,
                 tok_sm, slot_sm, sems, a0_ref, a1_ref, *lists_and_partials, tb):
    c = c_ref[0]
    b = pl.program_id(0)
    ngs = tb * PEER_TGROUPS
    e_sms = lists_and_partials[0:PEER_G]
    g_sms = lists_and_partials[PEER_G:2 * PEER_G]
    p_refs = lists_and_partials[2 * PEER_G:]
    smax = rec_hbm.shape[1]
    par = b & 1

    def list_copies(blk, half):
        rec_dst = pl.ds(half * smax, smax)
        cps = [pltpu.make_async_copy(rec_hbm.at[blk * SUBLANES + c], tok_sm.at[rec_dst], sems.at[half, 0]),
               pltpu.make_async_copy(rec_hbm.at[blk * SUBLANES + PEER_NCHUNK + c], slot_sm.at[rec_dst],
                                     sems.at[half, 1])]
        for k in range(PEER_G):
            dst = pl.ds(half * ngs, ngs)
            cps.append(pltpu.make_async_copy(e_hbm.at[k, pl.ds(blk * ngs, ngs)], e_sms[k].at[dst],
                                             sems.at[half, 2 + k]))
            cps.append(pltpu.make_async_copy(g_hbm.at[k, pl.ds(blk * ngs, ngs)], g_sms[k].at[dst],
                                             sems.at[half, 2 + PEER_G + k]))
        return cps

    @pl.when(b == 0)
    def _():
        for cp in list_copies(b, par):
            cp.start()

    @pl.when(b + 1 < pl.num_programs(0))
    def _():
        for cp in list_copies(b + 1, 1 - par):
            cp.start()

    for cp in list_copies(b, par):
        cp.wait()
    n = cnt_ref[b * PEER_NCHUNK + c]
    rec0 = par * smax

    def tree_sum(xs):
        while len(xs) > 1:
            xs = [xs[i] + xs[i + 1] for i in range(0, len(xs) - 1, 2)] + ([xs[-1]] if len(xs) % 2 else [])
        return xs[0]

    w = PEER_UNROLL
    nb = n // w
    last = jnp.maximum(nb - 1, 0)

    def decode(batch):
        s0 = rec0 + batch * w
        return tuple(tok_sm[s0 + j] for j in range(w)), tuple(slot_sm[s0 + j] for j in range(w))

    def scores(groups):
        toks, gss = groups
        width = len(toks)
        for j in range(width):
            t_lo = t_ref[toks[j], 0:SUBLANES, :]
            t_hi = t_ref[toks[j], SUBLANES:, :]
            for k in range(PEER_G):
                u = u_ref[e_sms[k][gss[j]]].astype(F32)
                p_refs[j][k * SUBLANES:(k + 1) * SUBLANES, :] = t_lo * u[0:SUBLANES] + t_hi * u[SUBLANES:]
        rs = [tree_sum([p_refs[j][pl.ds(i, PEER_G, stride=SUBLANES), :] for i in range(SUBLANES)])
              for j in range(width)]
        return jnp.concatenate(rs, axis=0)

    def activate(partials):
        score = jnp.sum(partials, axis=-1, keepdims=True)
        return jnp.broadcast_to(jax.nn.gelu(score), partials.shape)

    def store_act(a_ref, act):
        a_ref[...] = act

    def update(a_ref, groups, carry):
        prev_tok, acc_lo, acc_hi = carry
        toks, gss = groups
        for j in range(len(toks)):
            lo, hi = [], []
            for k in range(PEER_G):
                pair = j * PEER_G + k
                a = a_ref[pair:pair + 1, :] * g_sms[k][gss[j]]
                v = v_ref[e_sms[k][gss[j]]].astype(F32)
                lo.append(a * v[0:SUBLANES])
                hi.append(a * v[SUBLANES:])
            new = toks[j] != prev_tok
            acc_lo = jnp.where(new, acc_ref[toks[j], 0:SUBLANES, :], acc_lo) + tree_sum(lo)
            acc_hi = jnp.where(new, acc_ref[toks[j], SUBLANES:, :], acc_hi) + tree_sum(hi)
            o_ref[toks[j], 0:SUBLANES, :] = acc_lo
            o_ref[toks[j], SUBLANES:, :] = acc_hi
            prev_tok = toks[j]
        return prev_tok, acc_lo, acc_hi

    zero = jnp.zeros((SUBLANES, LANES), F32)
    store_act(a0_ref, activate(scores(decode(0))))
    partials = scores(decode(jnp.minimum(1, last)))

    def step(i, a_store, a_load, partials, cr):
        act = activate(partials)
        partials = scores(decode(jnp.minimum(i, last)))
        cr = update(a_load, decode(i - 2), cr)
        store_act(a_store, act)
        return partials, cr

    def two_steps(h, state):
        partials, cr = state
        i = 2 * h + 2
        partials, cr = step(i, a1_ref, a0_ref, partials, cr)
        partials, cr = step(i + 1, a0_ref, a1_ref, partials, cr)
        return partials, cr

    lax.fori_loop(0, nb // 2, two_steps, (partials, (jnp.int32(-1), zero, zero)))


def _peer(e_flat, g_flat, rec, cnt, t3, u3, v3, acc3):
    t = t3.shape[0]
    tb = min(PEER_TB, t)
    kern = functools.partial(_peer_kernel, tb=tb)
    tok_spec = pl.BlockSpec((tb, ROW_TILES, LANES), lambda b, c, n: (b, 0, 0))
    tab_spec = pl.BlockSpec((PEER_EC, ROW_TILES, LANES), lambda b, c, n: (c[0], 0, 0),
                            pipeline_mode=pl.Buffered(1))
    call = pl.pallas_call(
        kern,
        grid_spec=pltpu.PrefetchScalarGridSpec(
            num_scalar_prefetch=2,
            grid=(t // tb,),
            in_specs=[pl.BlockSpec(memory_space=pl.ANY), pl.BlockSpec(memory_space=pl.ANY),
                      pl.BlockSpec(memory_space=pl.ANY), tok_spec, tab_spec, tab_spec, tok_spec],
            out_specs=tok_spec,
            scratch_shapes=[
                pltpu.SMEM((2 * rec.shape[1],), jnp.int32),
                pltpu.SMEM((2 * rec.shape[1],), jnp.int32),
                pltpu.SemaphoreType.DMA((2, 2 + 2 * PEER_G)),
                pltpu.VMEM((PEER_UNROLL * PEER_G, LANES), F32),
                pltpu.VMEM((PEER_UNROLL * PEER_G, LANES), F32),
            ] + [pltpu.SMEM((2 * tb * PEER_TGROUPS,), jnp.int32)] * PEER_G
              + [pltpu.SMEM((2 * tb * PEER_TGROUPS,), F32)] * PEER_G
              + [pltpu.VMEM((PEER_G * SUBLANES, LANES), F32)] * PEER_UNROLL,
        ),
        out_shape=jax.ShapeDtypeStruct(acc3.shape, F32),
        input_output_aliases={8: 0},
        compiler_params=_cparams(("arbitrary",)),
        name="peer",
    )
    acc = acc3
    for c in range(PEER_NCHUNK):
        acc = call(jnp.full((1,), c, jnp.int32), cnt, e_flat, g_flat, rec, t3, u3, v3, acc)
    return acc


def _final_kernel(a_ref, g_ref, o_ref):
    a = a_ref[...].reshape(o_ref.shape)
    ms = jnp.mean(a * a, axis=-1, keepdims=True)
    o_ref[...] = a * lax.rsqrt(ms + EPS) * g_ref[...]


def _final(acc2, g):
    t = acc2.shape[0]
    tm = min(FINAL_TM, t)
    return pl.pallas_call(
        _final_kernel,
        grid=(t // tm,),
        in_specs=[pl.BlockSpec((tm, ROW_TILES, LANES), lambda i: (i, 0, 0)),
                  pl.BlockSpec((1, D_MODEL), lambda i: (0, 0))],
        out_specs=pl.BlockSpec((tm, D_MODEL), lambda i: (i, 0)),
        out_shape=jax.ShapeDtypeStruct((t, D_MODEL), F32),
        compiler_params=_cparams(("arbitrary",)),
        name="final",
    )(acc2, g)


def _layer(x, norm1_g, w_in, lam_re, lam_im, log_dt, b_re, b_im, c_re, c_im, d_skip, w_glu, conv_w, conv_b,
           w_q, w_k, w_v, b_i, b_f, mh_gain, mlstm_skip, w_mlstm_out, w_out, norm2_g, w_query, key1, key2,
           expert_u, expert_v):
    nb, seq, d = x.shape
    t = nb * seq
    x2 = x.reshape(t, d)
    row = lambda a: a.reshape(1, -1).astype(F32)

    o_ssm, o_xm, o_z, o_i, o_f, o_ga, o_gb = 0, 512, 2048, 3584, 3588, 3592, 5640
    zeros = lambda n: jnp.zeros((d, n), w_in.dtype)
    w_cat = jnp.concatenate([
        w_in[:, o_xm:o_z], w_in[:, o_z:o_i], w_in[:, o_ssm:o_xm], w_in[:, o_i:o_ga],
        zeros(COL_GA - COL_GATE - 2 * HEADS), w_in[:, o_ga:o_gb], w_in[:, o_gb:]], axis=1).astype(BF16)
    proj = _inproj(x2, row(norm1_g), w_cat)
    proj3 = proj.reshape(nb, seq, PROJ_W)

    bblk, a_re, a_im, cblk = _s5_params(lam_re, lam_im, log_dt, b_re, b_im, c_re, c_im)
    y_a = _s5(proj3, bblk, a_re, a_im, cblk, row(d_skip), w_glu.astype(BF16))

    gate_bias = jnp.concatenate([b_i.astype(F32), b_f.astype(F32), jnp.zeros((LANES - 2 * HEADS,), F32)]).reshape(1, LANES)
    y_b = _mlstm(proj3, conv_w.astype(F32), row(conv_b), w_q.astype(BF16), w_k.astype(BF16), w_v.astype(BF16),
                 gate_bias, row(mh_gain), row(mlstm_skip), w_mlstm_out.astype(BF16))

    h1, hn2, qr = _merge(proj, y_a.reshape(t, d), y_b.reshape(t, d), x2, w_out.astype(BF16), row(norm2_g),
                         w_query.astype(BF16))

    e_t, g_t, rec, cnt8 = _route(qr, key1.astype(BF16), key2.astype(BF16))
    tb = min(PEER_TB, t)
    by_pair = lambda a: a.reshape(PEER_G, PEER_TGROUPS, t // tb, tb).transpose(0, 2, 1, 3).reshape(PEER_G, -1)
    e_flat = by_pair(e_t)
    g_flat = by_pair(g_t)
    cnt = cnt8[:, 0].reshape(-1, SUBLANES)[:, :PEER_NCHUNK].reshape(-1)

    dense = lambda a: a.reshape(a.shape[0], ROW_TILES, LANES)
    return _peer(e_flat, g_flat, rec, cnt, hn2, dense(expert_u.astype(BF16)), dense(expert_v.astype(BF16)), h1)


def kernel(x, norm1_g, w_in, lam_re, lam_im, log_dt, b_re, b_im, c_re, c_im, d_skip, w_glu, conv_w, conv_b, w_q, w_k, w_v, b_i, b_f, mh_gain, mlstm_skip, w_mlstm_out, w_out, norm2_g, w_query, key1, key2, expert_u, expert_v, norm_f_g):
    depth = w_in.shape[0]
    nb, seq, d = x.shape
    h = x
    for l in range(depth):
        acc = _layer(h, norm1_g[l], w_in[l], lam_re[l], lam_im[l], log_dt[l], b_re[l], b_im[l], c_re[l], c_im[l],
                     d_skip[l], w_glu[l], conv_w[l], conv_b[l], w_q[l], w_k[l], w_v[l], b_i[l], b_f[l], mh_gain[l],
                     mlstm_skip[l], w_mlstm_out[l], w_out[l], norm2_g[l], w_query[l], key1[l], key2[l],
                     expert_u[l], expert_v[l])
        h = acc.reshape(nb, seq, d) if l + 1 < depth else None
    out = _final(acc, norm_f_g.reshape(1, d).astype(F32))
    return out.reshape(nb, seq, d)
```
